```python
import math
import jax, jax.numpy as jnp
from jax import lax
import numpy as np

D_MODEL = 1024
BATCH = 16
SEQ = 4096
DEPTH = 1

CHUNK = 64
RET_HEADS = 4
RET_HEAD_DIM = 256
RET_WIDTH = RET_HEADS * RET_HEAD_DIM
ROPE_THETA = 10000.0
CONV_WIDTH = D_MODEL
CONV_TAPS = 31
N_BRANCHES = 2
IN_COLS = 4 * RET_WIDTH + 2 * CONV_WIDTH + N_BRANCHES * D_MODEL
PEER_HEADS = 8
PEER_QUERY_DIM = 256
PEER_HALF = PEER_QUERY_DIM // 2
N_KEYS = 128
N_EXPERTS = N_KEYS * N_KEYS
PEER_TOPK = 16
PEER_BLOCK = 128
DEEPNORM_ALPHA = (2.0 * DEPTH) ** 0.25
DEEPNORM_BETA = (8.0 * DEPTH) ** -0.25
LN_EPS = 1e-5

kernel_name = 'hybrid_retention_conformer_peer_block'


def _layer_norm(x, gain=None, bias=None):
    xf = x.astype(jnp.float32)
    mu = jnp.mean(xf, axis=-1, keepdims=True)
    var = jnp.mean(jnp.square(xf - mu), axis=-1, keepdims=True)
    y = (xf - mu) * lax.rsqrt(var + LN_EPS)
    if gain is not None:
        y = y * gain.astype(jnp.float32) + bias.astype(jnp.float32)
    return y.astype(x.dtype)


def _rotary(t, positions):
    half = t.shape[-1] // 2
    inv_freq = ROPE_THETA ** (-jnp.arange(half, dtype=jnp.float32) / half)
    ang = positions.astype(jnp.float32)[..., None] * inv_freq
    cos = jnp.cos(ang)[:, :, None, :]
    sin = jnp.sin(ang)[:, :, None, :]
    tf = t.astype(jnp.float32)
    t1, t2 = tf[..., :half], tf[..., half:]
    out = jnp.concatenate([t1 * cos - t2 * sin, t1 * sin + t2 * cos], axis=-1)
    return out.astype(t.dtype)


def _retention(q, k, v):
    B, S, H, Dh = q.shape
    n_chunks = S // CHUNK
    in_dtype = q.dtype
    q, k, v = (t.astype(jnp.float32) for t in (q, k, v))
    log_gamma = jnp.log(1.0 - 2.0 ** (-5.0 - jnp.arange(H, dtype=jnp.float32)))
    pos = jnp.arange(CHUNK, dtype=jnp.float32)
    rel = jnp.abs(pos[:, None] - pos[None, :])
    intra_decay = jnp.exp(log_gamma[:, None, None] * rel)
    q_decay = jnp.exp(log_gamma[None, :] * (pos[:, None] + 1.0))
    k_decay = jnp.exp(log_gamma[None, :] * (CHUNK - 1.0 - pos[:, None]))
    chunk_decay = jnp.exp(log_gamma * CHUNK)

    def step(state, inp):
        qi, ki, vi = inp
        s = jnp.einsum('bqhd,bkhd->bhqk', qi, ki) * intra_decay
        intra = jnp.einsum('bhqk,bkhd->bqhd', s, vi)
        cross = jnp.einsum('bqhd,bhde->bqhe', qi, state) * q_decay[None, :, :, None]
        new_state = state * chunk_decay[None, :, None, None] + jnp.einsum(
            'bkhd,kh,bkhe->bhde', ki, k_decay, vi)
        return new_state, intra + cross

    xs = tuple(t.reshape(B, n_chunks, CHUNK, H, Dh).swapaxes(0, 1) for t in (q, k, v))
    init = jnp.zeros((B, H, Dh, Dh), jnp.float32)
    _, out = lax.scan(step, init, xs)
    return out.swapaxes(0, 1).reshape(B, S, H, Dh).astype(in_dtype)


def _conv_module(a, dw, dw_b, ln_g, ln_b, w_pw2, b_pw2):
    u, g = jnp.split(a, 2, axis=-1)
    y = u * jax.nn.sigmoid(g)
    y = jnp.pad(y, ((0, 0), (CONV_TAPS - 1, 0), (0, 0)))
    y = lax.conv_general_dilated(
        y, dw[:, None, :].astype(y.dtype), window_strides=(1,), padding='VALID',
        dimension_numbers=('NWC', 'WIO', 'NWC'), feature_group_count=CONV_WIDTH) + dw_b
    y = jax.nn.silu(_layer_norm(y, ln_g, ln_b))
    return y @ w_pw2 + b_pw2


def _mixer(h, positions, w_in, conv_dw, conv_dw_b, conv_ln_g, conv_ln_b,
           w_conv_out, b_conv_out, w_out):
    B, S, _ = h.shape
    proj = h @ w_in
    cuts = [RET_WIDTH, 2 * RET_WIDTH, 3 * RET_WIDTH, 4 * RET_WIDTH,
            4 * RET_WIDTH + 2 * CONV_WIDTH, 4 * RET_WIDTH + 2 * CONV_WIDTH + D_MODEL]
    q, k, v, g_ret, conv_in, gate_a, gate_b = jnp.split(proj, cuts, axis=-1)
    hd = (B, S, RET_HEADS, RET_HEAD_DIM)
    q = _rotary(q.reshape(hd), positions)
    k = _rotary(k.reshape(hd), positions) * (RET_HEAD_DIM ** -0.5)
    ret = _layer_norm(_retention(q, k, v.reshape(hd)))
    ret = ret.reshape(B, S, RET_WIDTH) * jax.nn.silu(g_ret)
    conv = _conv_module(conv_in, conv_dw, conv_dw_b, conv_ln_g, conv_ln_b,
                        w_conv_out, b_conv_out)
    merged = jax.nn.sigmoid(gate_a) * ret + jax.nn.sigmoid(gate_b) * conv
    return merged @ w_out


def _peer(h, wq, subkeys, u_tab, v_tab):
    B, S, D = h.shape
    tokens = h.reshape(-1, PEER_BLOCK, D)

    def block(xb):
        blk = xb.shape[0]
        q = (xb @ wq).reshape(blk, PEER_HEADS, 2, PEER_HALF)
        s = jnp.einsum('thpd,hpkd->thpk', q, subkeys).astype(jnp.float32)
        top_s, top_i = lax.top_k(s, PEER_TOPK)
        cand_s = top_s[:, :, 0, :, None] + top_s[:, :, 1, None, :]
        cand_i = top_i[:, :, 0, :, None] * N_KEYS + top_i[:, :, 1, None, :]
        cand_s = cand_s.reshape(blk, PEER_HEADS, PEER_TOPK * PEER_TOPK)
        cand_i = cand_i.reshape(blk, PEER_HEADS, PEER_TOPK * PEER_TOPK)
        best_s, best_pos = lax.top_k(cand_s, PEER_TOPK)
        expert = jnp.take_along_axis(cand_i, best_pos, axis=-1)
        gate = jax.nn.softmax(best_s, axis=-1).astype(xb.dtype)
        u = u_tab[expert]
        act = jax.nn.gelu(jnp.einsum('td,thed->the', xb, u), approximate=False)
        v = v_tab[expert]
        return jnp.einsum('the,thed->td', gate * act, v)

    out = lax.map(block, tokens)
    return out.reshape(B, S, D)


def setup_inputs(seed: int = 0) -> dict:
    key = jax.random.key(seed)
    ks = jax.random.split(key, 24)
    f32 = jnp.float32

    def nrm(k, shape, scale):
        return jax.random.normal(k, shape, f32) * scale

    x = nrm(ks[0], (BATCH, SEQ, D_MODEL), 1.0)
    c = nrm(ks[1], (BATCH, D_MODEL), 1.0)
    offset = jax.random.randint(ks[2], (BATCH, 1), 0, 1024, dtype=jnp.int32)
    positions = offset + jnp.arange(SEQ, dtype=jnp.int32)[None, :]
    return {
        'x': x,
        'c': c,
        'positions': positions,
        'w_ada': nrm(ks[3], (DEPTH, D_MODEL, 6 * D_MODEL), 0.5 * D_MODEL ** -0.5),
        'b_ada': nrm(ks[4], (DEPTH, 6 * D_MODEL), 0.02),
        'w_in': nrm(ks[5], (DEPTH, D_MODEL, IN_COLS), D_MODEL ** -0.5),
        'conv_dw': nrm(ks[6], (DEPTH, CONV_TAPS, CONV_WIDTH), CONV_TAPS ** -0.5),
        'conv_dw_b': nrm(ks[7], (DEPTH, CONV_WIDTH), 0.02),
        'conv_ln_g': 1.0 + nrm(ks[8], (DEPTH, CONV_WIDTH), 0.02),
        'conv_ln_b': nrm(ks[9], (DEPTH, CONV_WIDTH), 0.02),
        'w_conv_out': nrm(ks[10], (DEPTH, CONV_WIDTH, D_MODEL), CONV_WIDTH ** -0.5),
        'b_conv_out': nrm(ks[11], (DEPTH, D_MODEL), 0.02),
        'w_out': nrm(ks[12], (DEPTH, D_MODEL, D_MODEL), DEEPNORM_BETA * D_MODEL ** -0.5),
        'ln1_g': 1.0 + nrm(ks[13], (DEPTH, D_MODEL), 0.02),
        'ln1_b': nrm(ks[14], (DEPTH, D_MODEL), 0.02),
        'peer_wq': nrm(ks[15], (DEPTH, D_MODEL, PEER_HEADS * PEER_QUERY_DIM), D_MODEL ** -0.5),
        'peer_subkeys': nrm(ks[16], (DEPTH, PEER_HEADS, 2, N_KEYS, PEER_HALF), PEER_HALF ** -0.5),
        'peer_u': nrm(ks[17], (DEPTH, N_EXPERTS, D_MODEL), D_MODEL ** -0.5),
        'peer_v': nrm(ks[18], (DEPTH, N_EXPERTS, D_MODEL), DEEPNORM_BETA * PEER_HEADS ** -0.5),
        'ln2_g': 1.0 + nrm(ks[19], (DEPTH, D_MODEL), 0.02),
        'ln2_b': nrm(ks[20], (DEPTH, D_MODEL), 0.02),
    }


def reference(x, c, positions, w_ada, b_ada, w_in, conv_dw, conv_dw_b, conv_ln_g,
              conv_ln_b, w_conv_out, b_conv_out, w_out, ln1_g, ln1_b, peer_wq,
              peer_subkeys, peer_u, peer_v, ln2_g, ln2_b):
    cond = jax.nn.silu(c)
    for l in range(DEPTH):
        mod = cond @ w_ada[l] + b_ada[l]
        sh1, sc1, g1, sh2, sc2, g2 = jnp.split(mod, 6, axis=-1)
        h1 = _layer_norm(x) * (1.0 + sc1[:, None, :]) + sh1[:, None, :]
        mix = _mixer(h1, positions, w_in[l], conv_dw[l], conv_dw_b[l], conv_ln_g[l],
                     conv_ln_b[l], w_conv_out[l], b_conv_out[l], w_out[l])
        x = _layer_norm(DEEPNORM_ALPHA * x + g1[:, None, :] * mix, ln1_g[l], ln1_b[l])
        h2 = _layer_norm(x) * (1.0 + sc2[:, None, :]) + sh2[:, None, :]
        ffn = _peer(h2, peer_wq[l], peer_subkeys[l], peer_u[l], peer_v[l])
        x = _layer_norm(DEEPNORM_ALPHA * x + g2[:, None, :] * ffn, ln2_g[l], ln2_b[l])
    return x
```

```python
import jax
import jax.numpy as jnp
from jax import lax
from jax.experimental import pallas as pl
from jax.experimental.pallas import tpu as pltpu

D_MODEL = 1024
CHUNK = 64
RET_HEADS = 4
RET_HEAD_DIM = 256
RET_WIDTH = RET_HEADS * RET_HEAD_DIM
ROPE_THETA = 10000.0
CONV_WIDTH = D_MODEL
CONV_TAPS = 31
PEER_HEADS = 8
PEER_HALF = 128
N_KEYS = 128
N_EXPERTS = N_KEYS * N_KEYS
PEER_TOPK = 16
PEER_PAIRS = PEER_HEADS * PEER_TOPK
DEPTH = 1
DEEPNORM_ALPHA = (2.0 * DEPTH) ** 0.25
LN_EPS = 1e-5

SUBLANES = 8
LANES = 128
ROWS_PER_EXPERT = D_MODEL // (2 * LANES)
TAB_ROWS = N_EXPERTS * ROWS_PER_EXPERT + SUBLANES
HI_MASK = -65536
TOK_BLOCK = 128
PEER_VMEM_LIMIT = 52 * 1024 * 1024


def _layer_norm(x, gain=None, bias=None):
    mu = jnp.mean(x, axis=-1, keepdims=True)
    var = jnp.mean(jnp.square(x - mu), axis=-1, keepdims=True)
    y = (x - mu) * lax.rsqrt(var + LN_EPS)
    if gain is not None:
        y = y * gain + bias
    return y


def _rotary(t, positions):
    half = t.shape[-1] // 2
    inv_freq = ROPE_THETA ** (-jnp.arange(half, dtype=jnp.float32) / half)
    ang = positions.astype(jnp.float32)[..., None] * inv_freq
    cos = jnp.cos(ang)[:, :, None, :]
    sin = jnp.sin(ang)[:, :, None, :]
    t1, t2 = t[..., :half], t[..., half:]
    return jnp.concatenate([t1 * cos - t2 * sin, t1 * sin + t2 * cos], axis=-1)


def _retention(q, k, v):
    B, S, H, Dh = q.shape
    n_chunks = S // CHUNK
    log_gamma = jnp.log(1.0 - 2.0 ** (-5.0 - jnp.arange(H, dtype=jnp.float32)))
    pos = jnp.arange(CHUNK, dtype=jnp.float32)
    rel = jnp.abs(pos[:, None] - pos[None, :])
    intra_decay = jnp.exp(log_gamma[:, None, None] * rel)
    q_decay = jnp.exp(log_gamma[None, :] * (pos[:, None] + 1.0))
    k_decay = jnp.exp(log_gamma[None, :] * (CHUNK - 1.0 - pos[:, None]))
    chunk_decay = jnp.exp(log_gamma * CHUNK)

    def step(state, inp):
        qi, ki, vi = inp
        s = jnp.einsum('bqhd,bkhd->bhqk', qi, ki) * intra_decay
        intra = jnp.einsum('bhqk,bkhd->bqhd', s, vi)
        cross = jnp.einsum('bqhd,bhde->bqhe', qi, state) * q_decay[None, :, :, None]
        new_state = state * chunk_decay[None, :, None, None] + jnp.einsum(
            'bkhd,kh,bkhe->bhde', ki, k_decay, vi)
        return new_state, intra + cross

    xs = tuple(t.reshape(B, n_chunks, CHUNK, H, Dh).swapaxes(0, 1) for t in (q, k, v))
    init = jnp.zeros((B, H, Dh, Dh), jnp.float32)
    _, out = lax.scan(step, init, xs)
    return out.swapaxes(0, 1).reshape(B, S, H, Dh)


def _conv_module(a, dw, dw_b, ln_g, ln_b, w_pw2, b_pw2):
    u, g = jnp.split(a, 2, axis=-1)
    y = u * jax.nn.sigmoid(g)
    y = jnp.pad(y, ((0, 0), (CONV_TAPS - 1, 0), (0, 0)))
    y = lax.conv_general_dilated(
        y, dw[:, None, :], window_strides=(1,), padding='VALID',
        dimension_numbers=('NWC', 'WIO', 'NWC'), feature_group_count=CONV_WIDTH) + dw_b
    y = jax.nn.silu(_layer_norm(y, ln_g, ln_b))
    return y @ w_pw2 + b_pw2


def _mixer(h, positions, w_in, conv_dw, conv_dw_b, conv_ln_g, conv_ln_b,
           w_conv_out, b_conv_out, w_out):
    B, S, _ = h.shape
    proj = h @ w_in
    cuts = [RET_WIDTH, 2 * RET_WIDTH, 3 * RET_WIDTH, 4 * RET_WIDTH,
            4 * RET_WIDTH + 2 * CONV_WIDTH, 4 * RET_WIDTH + 2 * CONV_WIDTH + D_MODEL]
    q, k, v, g_ret, conv_in, gate_a, gate_b = jnp.split(proj, cuts, axis=-1)
    hd = (B, S, RET_HEADS, RET_HEAD_DIM)
    q = _rotary(q.reshape(hd), positions)
    k = _rotary(k.reshape(hd), positions) * (RET_HEAD_DIM ** -0.5)
    ret = _layer_norm(_retention(q, k, v.reshape(hd)))
    ret = ret.reshape(B, S, RET_WIDTH) * jax.nn.silu(g_ret)
    conv = _conv_module(conv_in, conv_dw, conv_dw_b, conv_ln_g, conv_ln_b,
                        w_conv_out, b_conv_out)
    merged = jax.nn.sigmoid(gate_a) * ret + jax.nn.sigmoid(gate_b) * conv
    return merged @ w_out


def _topk_rows(s, k):
    n_rows = s.shape[0]
    row = lax.broadcasted_iota(jnp.int32, s.shape, 0)
    vals, idxs = [], []
    for _ in range(k):
        m = jnp.max(s, axis=0, keepdims=True)
        i = jnp.min(jnp.where(s == m, row, n_rows), axis=0, keepdims=True)
        vals.append(m)
        idxs.append(i)
        s = jnp.where(row == i, -jnp.inf, s)
    return jnp.concatenate(vals, axis=0), jnp.concatenate(idxs, axis=0)


def _router_kernel(h_ref, wqt_ref, keys_ref, idx_ref, gate_ref):
    hb = h_ref[...].astype(jnp.bfloat16)
    qt = lax.dot_general(wqt_ref[...], hb, (((1,), (1,)), ((), ())),
                         preferred_element_type=jnp.float32)
    cand_row = lax.broadcasted_iota(jnp.int32, (PEER_TOPK * PEER_TOPK, TOK_BLOCK), 0)
    all_e, all_gate = [], []
    for h in range(PEER_HEADS):
        tops = []
        for p in range(2):
            hp = 2 * h + p
            qhp = qt[hp * PEER_HALF:(hp + 1) * PEER_HALF, :].astype(jnp.bfloat16)
            s = jnp.dot(keys_ref[hp], qhp, preferred_element_type=jnp.float32)
            tops.append(_topk_rows(s, PEER_TOPK))
        (s0, i0), (s1, i1) = tops
        cand_s = jnp.concatenate([s0[i:i + 1, :] + s1 for i in range(PEER_TOPK)], axis=0)
        cand_e = jnp.concatenate([i0[i:i + 1, :] * N_KEYS + i1 for i in range(PEER_TOPK)], axis=0)
        best_s, best_e = [], []
        for _ in range(PEER_TOPK):
            m = jnp.max(cand_s, axis=0, keepdims=True)
            pos = jnp.min(jnp.where(cand_s == m, cand_row, PEER_TOPK * PEER_TOPK), axis=0, keepdims=True)
            hit = cand_row == pos
            best_s.append(m)
            best_e.append(jnp.max(jnp.where(hit, cand_e, -1), axis=0, keepdims=True))
            cand_s = jnp.where(hit, -jnp.inf, cand_s)
        best_s = jnp.concatenate(best_s, axis=0)
        ex = jnp.exp(best_s - best_s[0:1, :])
        all_gate.append(ex / jnp.sum(ex, axis=0, keepdims=True))
        all_e.append(jnp.concatenate(best_e, axis=0) * ROWS_PER_EXPERT)
    idx_ref[...] = jnp.concatenate(all_e, axis=0).T
    gate_ref[...] = jnp.concatenate(all_gate, axis=0).T


def _peer_router(h, wqt, keys):
    T = h.shape[0]
    return pl.pallas_call(
        _router_kernel,
        grid=(T // TOK_BLOCK,),
        in_specs=[
            pl.BlockSpec((TOK_BLOCK, D_MODEL), lambda i: (i, 0)),
            pl.BlockSpec(wqt.shape, lambda i: (0, 0)),
            pl.BlockSpec(keys.shape, lambda i: (0, 0, 0)),
        ],
        out_specs=[
            pl.BlockSpec((TOK_BLOCK, PEER_PAIRS), lambda i: (i, 0)),
            pl.BlockSpec((TOK_BLOCK, PEER_PAIRS), lambda i: (i, 0)),
        ],
        out_shape=[
            jax.ShapeDtypeStruct((T, PEER_PAIRS), jnp.int32),
            jax.ShapeDtypeStruct((T, PEER_PAIRS), jnp.float32),
        ],
        compiler_params=pltpu.CompilerParams(dimension_semantics=("arbitrary",)),
        name="peer_router",
    )(h, wqt, keys)


def _pack_table(tab):
    n = tab.shape[0]
    b = lax.bitcast_convert_type(tab.astype(jnp.bfloat16), jnp.uint16).astype(jnp.uint32)
    b = b.reshape(n, 2, ROWS_PER_EXPERT, LANES)
    packed = lax.bitcast_convert_type(b[:, 0] | (b[:, 1] << 16), jnp.int32)
    packed = packed.reshape(n * ROWS_PER_EXPERT, LANES)
    return jnp.concatenate([packed, jnp.zeros((SUBLANES, LANES), jnp.int32)], axis=0)


def _unpack(c):
    lo = lax.bitcast_convert_type(c << 16, jnp.float32)
    hi = lax.bitcast_convert_type(c & HI_MASK, jnp.float32)
    return lo, hi


def _sublane_masks():
    sub = lax.broadcasted_iota(jnp.int32, (SUBLANES, LANES), 0)
    return sub < 4, (sub & 2) == 0, (sub & 1) == 0


def _peer_dots_kernel(idx_ref, x_ref, gate_ref, tab_ref, w_ref, q_buf, d_buf):
    m4, m2, m1 = _sublane_masks()
    lane = lax.broadcasted_iota(jnp.int32, (SUBLANES, TOK_BLOCK), 1)
    groups = [slice(g * SUBLANES, (g + 1) * SUBLANES) for g in range(PEER_PAIRS // SUBLANES)]
    q_buf[1] = jnp.zeros((PEER_PAIRS, LANES), jnp.float32)
    d_buf[...] = jnp.zeros((PEER_PAIRS, TOK_BLOCK), jnp.float32)

    def step(t, carry):
        for rows in groups:
            col = jnp.sum(q_buf[(t + 1) & 1, rows, :], axis=1, keepdims=True)
            d_buf[rows, :] = jnp.where(lane == t - 1, col, d_buf[rows, :])

        tok = jnp.minimum(t, TOK_BLOCK - 1)
        xt = x_ref[tok]
        xr = pltpu.roll(xt, 4, axis=0)
        xlo = jnp.where(m4, xt, xr)
        xhi = jnp.where(m4, xr, xt)

        def prod(pa, pb):
            wa = tab_ref[pl.ds(idx_ref[tok, pa], SUBLANES), :]
            wb = tab_ref[pl.ds(idx_ref[tok, pb], SUBLANES), :]
            c = jnp.where(m4, wa, pltpu.roll(wb, 4, axis=0))
            lo, hi = _unpack(c)
            return lo * xlo + hi * xhi

        for rows in groups:
            p = rows.start
            v = [prod(p + a, p + b) for a, b in ((0, 4), (2, 6), (1, 5), (3, 7))]
            w = [jnp.where(m2, v1, pltpu.roll(v2, 2, axis=0)) + jnp.where(m2, pltpu.roll(v1, 6, axis=0), v2)
                 for v1, v2 in ((v[0], v[1]), (v[2], v[3]))]
            q_buf[t & 1, rows, :] = (jnp.where(m1, w[0], pltpu.roll(w[1], 1, axis=0))
                                     + jnp.where(m1, pltpu.roll(w[0], 7, axis=0), w[1]))
        return carry

    lax.fori_loop(0, TOK_BLOCK + 1, step, 0)

    dots = d_buf[...].T
    act = 0.5 * dots * (1.0 + lax.erf(dots * (2.0 ** -0.5)))
    w_ref[...] = act * gate_ref[...]


def _peer_dots(idx, x, gate, tab):
    T = x.shape[0]
    return pl.pallas_call(
        _peer_dots_kernel,
        grid=(T // TOK_BLOCK,),
        in_specs=[
            pl.BlockSpec((TOK_BLOCK, PEER_PAIRS), lambda i: (i, 0), memory_space=pltpu.SMEM),
            pl.BlockSpec((TOK_BLOCK, SUBLANES, LANES), lambda i: (i, 0, 0)),
            pl.BlockSpec((TOK_BLOCK, PEER_PAIRS), lambda i: (i, 0)),
            pl.BlockSpec(memory_space=pltpu.VMEM),
        ],
        out_specs=pl.BlockSpec((TOK_BLOCK, PEER_PAIRS), lambda i: (i, 0)),
        out_shape=jax.ShapeDtypeStruct((T, PEER_PAIRS), jnp.float32),
        scratch_shapes=[
            pltpu.VMEM((2, PEER_PAIRS, LANES), jnp.float32),
            pltpu.VMEM((PEER_PAIRS, TOK_BLOCK), jnp.float32),
        ],
        compiler_params=pltpu.CompilerParams(
            dimension_semantics=("arbitrary",), vmem_limit_bytes=PEER_VMEM_LIMIT),
        name="peer_dots",
    )(idx, x.reshape(T, SUBLANES, LANES), gate, tab)


def _peer_mix_kernel(idx_ref, w_ref, tab_ref, out_ref, m_buf):
    m4, _, _ = _sublane_masks()
    eye = (lax.broadcasted_iota(jnp.int32, (PEER_PAIRS, PEER_PAIRS), 0)
           == lax.broadcasted_iota(jnp.int32, (PEER_PAIRS, PEER_PAIRS), 1))
    ones = jnp.ones((PEER_PAIRS, LANES), jnp.bfloat16)
    n_acc = 4

    def spread_weights(tok, slot):
        a = jnp.where(eye, w_ref[pl.ds(tok, 1), :], 0.0)
        a_hi = a.astype(jnp.bfloat16)
        a_lo = (a - a_hi.astype(jnp.float32)).astype(jnp.bfloat16)
        m_buf[slot] = (jnp.dot(a_hi, ones, preferred_element_type=jnp.float32)
                       + jnp.dot(a_lo, ones, preferred_element_type=jnp.float32))

    spread_weights(0, 0)

    def token(t, carry):
        acc_lo = [jnp.zeros((SUBLANES, LANES), jnp.float32) for _ in range(n_acc)]
        acc_hi = [jnp.zeros((SUBLANES, LANES), jnp.float32) for _ in range(n_acc)]
        for p in range(PEER_PAIRS):
            lo, hi = _unpack(tab_ref[pl.ds(idx_ref[t, p], SUBLANES), :])
            wt = jnp.broadcast_to(m_buf[t & 1, p:p + 1, :], (SUBLANES, LANES))
            acc_lo[p % n_acc] = acc_lo[p % n_acc] + wt * lo
            acc_hi[p % n_acc] = acc_hi[p % n_acc] + wt * hi
        lo = (acc_lo[0] + acc_lo[1]) + (acc_lo[2] + acc_lo[3])
        hi = (acc_hi[0] + acc_hi[1]) + (acc_hi[2] + acc_hi[3])
        out_ref[t] = jnp.where(m4, lo, pltpu.roll(hi, 4, axis=0))
        spread_weights(jnp.minimum(t + 1, TOK_BLOCK - 1), (t + 1) & 1)
        return carry

    lax.fori_loop(0, TOK_BLOCK, token, 0)


def _peer_mix(idx, w, tab):
    T = idx.shape[0]
    out = pl.pallas_call(
        _peer_mix_kernel,
        grid=(T // TOK_BLOCK,),
        in_specs=[
            pl.BlockSpec((TOK_BLOCK, PEER_PAIRS), lambda i: (i, 0), memory_space=pltpu.SMEM),
            pl.BlockSpec((TOK_BLOCK, PEER_PAIRS), lambda i: (i, 0)),
            pl.BlockSpec(memory_space=pltpu.VMEM),
        ],
        out_specs=pl.BlockSpec((TOK_BLOCK, SUBLANES, LANES), lambda i: (i, 0, 0)),
        out_shape=jax.ShapeDtypeStruct((T, SUBLANES, LANES), jnp.float32),
        scratch_shapes=[pltpu.VMEM((2, PEER_PAIRS, LANES), jnp.float32)],
        compiler_params=pltpu.CompilerParams(
            dimension_semantics=("arbitrary",), vmem_limit_bytes=PEER_VMEM_LIMIT),
        name="peer_mix",
    )(idx, w, tab)
    return out.reshape(T, D_MODEL)


def _peer(h, wq, subkeys, u_tab, v_tab):
    B, S, D = h.shape
    hf = h.reshape(B * S, D)
    wqt = wq.T.astype(jnp.bfloat16)
    keys = subkeys.reshape(PEER_HEADS * 2, N_KEYS, PEER_HALF).astype(jnp.bfloat16)
    idx, gate = _peer_router(hf, wqt, keys)
    w = _peer_dots(idx, hf, gate, _pack_table(u_tab))
    return _peer_mix(idx, w, _pack_table(v_tab)).reshape(B, S, D)


def _final_norm_kernel(x_ref, f_ref, g2_ref, gain_ref, bias_ref, o_ref):
    y = DEEPNORM_ALPHA * x_ref[0] + g2_ref[0] * f_ref[0]
    mu = jnp.mean(y, axis=-1, keepdims=True)
    var = jnp.mean(jnp.square(y - mu), axis=-1, keepdims=True)
    o_ref[0] = (y - mu) * lax.rsqrt(var + LN_EPS) * gain_ref[...] + bias_ref[...]


def _final_norm(x, ffn, g2, gain, bias):
    B, S, D = x.shape
    ts = 512
    return pl.pallas_call(
        _final_norm_kernel,
        grid=(B, S // ts),
        in_specs=[
            pl.BlockSpec((1, ts, D), lambda b, s: (b, s, 0)),
            pl.BlockSpec((1, ts, D), lambda b, s: (b, s, 0)),
            pl.BlockSpec((1, 1, D), lambda b, s: (b, 0, 0)),
            pl.BlockSpec((1, D), lambda b, s: (0, 0)),
            pl.BlockSpec((1, D), lambda b, s: (0, 0)),
        ],
        out_specs=pl.BlockSpec((1, ts, D), lambda b, s: (b, s, 0)),
        out_shape=jax.ShapeDtypeStruct((B, S, D), jnp.float32),
        name="final_norm",
    )(x, ffn, g2[:, None, :], gain[None, :], bias[None, :])


def kernel(x, c, positions, w_ada, b_ada, w_in, conv_dw, conv_dw_b, conv_ln_g, conv_ln_b, w_conv_out, b_conv_out, w_out, ln1_g, ln1_b, peer_wq, peer_subkeys, peer_u, peer_v, ln2_g, ln2_b):
    cond = jax.nn.silu(c)
    l = 0
    mod = cond @ w_ada[l] + b_ada[l]
    sh1, sc1, g1, sh2, sc2, g2 = jnp.split(mod, 6, axis=-1)
    h1 = _layer_norm(x) * (1.0 + sc1[:, None, :]) + sh1[:, None, :]
    mix = _mixer(h1, positions, w_in[l], conv_dw[l], conv_dw_b[l], conv_ln_g[l],
                 conv_ln_b[l], w_conv_out[l], b_conv_out[l], w_out[l])
    x = _layer_norm(DEEPNORM_ALPHA * x + g1[:, None, :] * mix, ln1_g[l], ln1_b[l])
    h2 = _layer_norm(x) * (1.0 + sc2[:, None, :]) + sh2[:, None, :]
    ffn = _peer(h2, peer_wq[l], peer_subkeys[l], peer_u[l], peer_v[l])
    return _final_norm(x, ffn, g2, ln2_g[l], ln2_b[l])
```

```python
import jax
import jax.numpy as jnp
from jax import lax
from jax.experimental import pallas as pl
from jax.experimental.pallas import tpu as pltpu

D_MODEL = 1024
CHUNK = 64
RET_HEADS = 4
RET_HEAD_DIM = 256
RET_WIDTH = RET_HEADS * RET_HEAD_DIM
ROPE_THETA = 10000.0
CONV_WIDTH = D_MODEL
CONV_TAPS = 31
PEER_HEADS = 8
PEER_HALF = 128
N_KEYS = 128
N_EXPERTS = N_KEYS * N_KEYS
PEER_TOPK = 16
PEER_PAIRS = PEER_HEADS * PEER_TOPK
DEPTH = 1
DEEPNORM_ALPHA = (2.0 * DEPTH) ** 0.25
LN_EPS = 1e-5

SUBLANES = 8
LANES = 128
ROWS_PER_EXPERT = D_MODEL // (2 * LANES)
TAB_FRONT = SUBLANES
HI_MASK = -65536
TOK_BLOCK = 128
IDX_BLOCK = PEER_PAIRS * TOK_BLOCK
MIX_UNROLL = 2
PEER_VMEM_LIMIT = 52 * 1024 * 1024


def _layer_norm(x, gain=None, bias=None):
    mu = jnp.mean(x, axis=-1, keepdims=True)
    var = jnp.mean(jnp.square(x - mu), axis=-1, keepdims=True)
    y = (x - mu) * lax.rsqrt(var + LN_EPS)
    if gain is not None:
        y = y * gain + bias
    return y


def _rotary(t, positions):
    half = t.shape[-1] // 2
    inv_freq = ROPE_THETA ** (-jnp.arange(half, dtype=jnp.float32) / half)
    ang = positions.astype(jnp.float32)[..., None] * inv_freq
    cos = jnp.cos(ang)[:, :, None, :]
    sin = jnp.sin(ang)[:, :, None, :]
    t1, t2 = t[..., :half], t[..., half:]
    return jnp.concatenate([t1 * cos - t2 * sin, t1 * sin + t2 * cos], axis=-1)


def _retention(q, k, v):
    B, S, H, Dh = q.shape
    n_chunks = S // CHUNK
    log_gamma = jnp.log(1.0 - 2.0 ** (-5.0 - jnp.arange(H, dtype=jnp.float32)))
    pos = jnp.arange(CHUNK, dtype=jnp.float32)
    rel = jnp.abs(pos[:, None] - pos[None, :])
    intra_decay = jnp.exp(log_gamma[:, None, None] * rel)
    q_decay = jnp.exp(log_gamma[None, :] * (pos[:, None] + 1.0))
    k_decay = jnp.exp(log_gamma[None, :] * (CHUNK - 1.0 - pos[:, None]))
    chunk_decay = jnp.exp(log_gamma * CHUNK)

    def step(state, inp):
        qi, ki, vi = inp
        s = jnp.einsum('bqhd,bkhd->bhqk', qi, ki) * intra_decay
        intra = jnp.einsum('bhqk,bkhd->bqhd', s, vi)
        cross = jnp.einsum('bqhd,bhde->bqhe', qi, state) * q_decay[None, :, :, None]
        new_state = state * chunk_decay[None, :, None, None] + jnp.einsum(
            'bkhd,kh,bkhe->bhde', ki, k_decay, vi)
        return new_state, intra + cross

    xs = tuple(t.reshape(B, n_chunks, CHUNK, H, Dh).swapaxes(0, 1) for t in (q, k, v))
    init = jnp.zeros((B, H, Dh, Dh), jnp.float32)
    _, out = lax.scan(step, init, xs)
    return out.swapaxes(0, 1).reshape(B, S, H, Dh)


def _conv_module(a, dw, dw_b, ln_g, ln_b, w_pw2, b_pw2):
    u, g = jnp.split(a, 2, axis=-1)
    y = u * jax.nn.sigmoid(g)
    y = jnp.pad(y, ((0, 0), (CONV_TAPS - 1, 0), (0, 0)))
    y = lax.conv_general_dilated(
        y, dw[:, None, :], window_strides=(1,), padding='VALID',
        dimension_numbers=('NWC', 'WIO', 'NWC'), feature_group_count=CONV_WIDTH) + dw_b
    y = jax.nn.silu(_layer_norm(y, ln_g, ln_b))
    return y @ w_pw2 + b_pw2


def _mixer(h, positions, w_in, conv_dw, conv_dw_b, conv_ln_g, conv_ln_b,
           w_conv_out, b_conv_out, w_out):
    B, S, _ = h.shape
    proj = h @ w_in
    cuts = [RET_WIDTH, 2 * RET_WIDTH, 3 * RET_WIDTH, 4 * RET_WIDTH,
            4 * RET_WIDTH + 2 * CONV_WIDTH, 4 * RET_WIDTH + 2 * CONV_WIDTH + D_MODEL]
    q, k, v, g_ret, conv_in, gate_a, gate_b = jnp.split(proj, cuts, axis=-1)
    hd = (B, S, RET_HEADS, RET_HEAD_DIM)
    q = _rotary(q.reshape(hd), positions)
    k = _rotary(k.reshape(hd), positions) * (RET_HEAD_DIM ** -0.5)
    ret = _layer_norm(_retention(q, k, v.reshape(hd)))
    ret = ret.reshape(B, S, RET_WIDTH) * jax.nn.silu(g_ret)
    conv = _conv_module(conv_in, conv_dw, conv_dw_b, conv_ln_g, conv_ln_b,
                        w_conv_out, b_conv_out)
    merged = jax.nn.sigmoid(gate_a) * ret + jax.nn.sigmoid(gate_b) * conv
    return merged @ w_out


def _topk_rows(s, k):
    n_rows = s.shape[0]
    row = lax.broadcasted_iota(jnp.int32, s.shape, 0)
    vals, idxs = [], []
    for _ in range(k):
        m = jnp.max(s, axis=0, keepdims=True)
        i = jnp.min(jnp.where(s == m, row, n_rows), axis=0, keepdims=True)
        vals.append(m)
        idxs.append(i)
        s = jnp.where(row == i, -jnp.inf, s)
    return jnp.concatenate(vals, axis=0), jnp.concatenate(idxs, axis=0)


def _candidate_tiles():
    k = PEER_TOPK
    tiles = []
    for j in range(SUBLANES):
        n_i = k // (j + 1)
        for i0 in range(0, n_i, SUBLANES):
            tiles.append((i0, min(SUBLANES, n_i - i0), j, None))
    tiles.append((0, 1, None, SUBLANES))
    return tiles


def _router_kernel(h_ref, wqt_ref, keys_ref, idx_ref, gate_ref):
    k = PEER_TOPK
    hb = h_ref[...].astype(jnp.bfloat16)
    qt = lax.dot_general(wqt_ref[...], hb, (((1,), (1,)), ((), ())),
                         preferred_element_type=jnp.float32)
    sub = lax.broadcasted_iota(jnp.int32, (SUBLANES, TOK_BLOCK), 0)
    tiles = _candidate_tiles()
    tile_pos = []
    for i0, n_i, j, j0 in tiles:
        if j is not None:
            tile_pos.append(jnp.where(sub < n_i, (sub + i0) * k + j, k * k))
        else:
            tile_pos.append(sub + j0)
    all_e, all_gate = [], []
    for h in range(PEER_HEADS):
        tops = []
        for p in range(2):
            hp = 2 * h + p
            qhp = qt[hp * PEER_HALF:(hp + 1) * PEER_HALF, :].astype(jnp.bfloat16)
            s = jnp.dot(keys_ref[hp], qhp, preferred_element_type=jnp.float32)
            tops.append(_topk_rows(s, k))
        (s0, i0v), (s1, i1v) = tops
        e0 = i0v * N_KEYS
        cand_s, cand_e = [], []
        for (i0, n_i, j, j0), pos in zip(tiles, tile_pos):
            if j is not None:
                cs = s0[i0:i0 + SUBLANES, :] + s1[j:j + 1, :]
                ce = e0[i0:i0 + SUBLANES, :] + i1v[j:j + 1, :]
                cand_s.append(jnp.where(pos < k * k, cs, -jnp.inf))
            else:
                cs = s0[0:1, :] + s1[j0:j0 + SUBLANES, :]
                ce = e0[0:1, :] + i1v[j0:j0 + SUBLANES, :]
                cand_s.append(cs)
            cand_e.append(ce)
        cand_s = jnp.concatenate(cand_s, axis=0)
        cand_e = jnp.concatenate(cand_e, axis=0)
        cand_pos = jnp.concatenate(tile_pos, axis=0)
        best_s, best_e = [], []
        for _ in range(k):
            m = jnp.max(cand_s, axis=0, keepdims=True)
            pos = jnp.min(jnp.where(cand_s == m, cand_pos, k * k), axis=0, keepdims=True)
            hit = cand_pos == pos
            best_s.append(m)
            best_e.append(jnp.max(jnp.where(hit, cand_e, -1), axis=0, keepdims=True))
            cand_s = jnp.where(hit, -jnp.inf, cand_s)
        best_s = jnp.concatenate(best_s, axis=0)
        ex = jnp.exp(best_s - best_s[0:1, :])
        all_gate.append(ex / jnp.sum(ex, axis=0, keepdims=True))
        all_e.append(jnp.concatenate(best_e, axis=0))
    pair = lax.broadcasted_iota(jnp.int32, (PEER_PAIRS, TOK_BLOCK), 0)
    first = (pair & 4) == 0
    idx_ref[...] = (jnp.concatenate(all_e, axis=0) * ROWS_PER_EXPERT
                    + jnp.where(first, TAB_FRONT, TAB_FRONT - ROWS_PER_EXPERT))
    gate_ref[...] = jnp.concatenate(all_gate, axis=0).T


def _peer_router(h, wqt, keys):
    T = h.shape[0]
    return pl.pallas_call(
        _router_kernel,
        grid=(T // TOK_BLOCK,),
        in_specs=[
            pl.BlockSpec((TOK_BLOCK, D_MODEL), lambda i: (i, 0)),
            pl.BlockSpec(wqt.shape, lambda i: (0, 0)),
            pl.BlockSpec(keys.shape, lambda i: (0, 0, 0)),
        ],
        out_specs=[
            pl.BlockSpec((PEER_PAIRS, TOK_BLOCK), lambda i: (i, 0)),
            pl.BlockSpec((TOK_BLOCK, PEER_PAIRS), lambda i: (i, 0)),
        ],
        out_shape=[
            jax.ShapeDtypeStruct((T // TOK_BLOCK * PEER_PAIRS, TOK_BLOCK), jnp.int32),
            jax.ShapeDtypeStruct((T, PEER_PAIRS), jnp.float32),
        ],
        compiler_params=pltpu.CompilerParams(dimension_semantics=("arbitrary",)),
        name="peer_router",
    )(h, wqt, keys)


def _pack_table(tab):
    n = tab.shape[0]
    b = lax.bitcast_convert_type(tab.astype(jnp.bfloat16), jnp.uint16).astype(jnp.uint32)
    b = b.reshape(n, 2, ROWS_PER_EXPERT, LANES)
    packed = lax.bitcast_convert_type(b[:, 0] | (b[:, 1] << 16), jnp.int32)
    packed = packed.reshape(n * ROWS_PER_EXPERT, LANES)
    pad = jnp.zeros((TAB_FRONT, LANES), jnp.int32)
    return jnp.concatenate([pad, packed, pad], axis=0)


def _unpack(c):
    lo = lax.bitcast_convert_type(c << 16, jnp.float32)
    hi = lax.bitcast_convert_type(c & HI_MASK, jnp.float32)
    return lo, hi


def _sublane_masks():
    sub = lax.broadcasted_iota(jnp.int32, (SUBLANES, LANES), 0)
    return sub < 4, (sub & 2) == 0, (sub & 1) == 0


def _idx_copy(idx_hbm, block, buf, sem):
    return pltpu.make_async_copy(idx_hbm.at[pl.ds(block * IDX_BLOCK, IDX_BLOCK)], buf, sem)


def _two_block_step(idx_hbm, idx_a, idx_b, sems, process):
    i = pl.program_id(0)

    @pl.when(i == 0)
    def _():
        _idx_copy(idx_hbm, 0, idx_a, sems.at[0]).start()

    _idx_copy(idx_hbm, 2 * i + 1, idx_b, sems.at[1]).start()
    _idx_copy(idx_hbm, 2 * i, idx_a, sems.at[0]).wait()
    process(idx_a, 0)

    @pl.when(i + 1 < pl.num_programs(0))
    def _():
        _idx_copy(idx_hbm, 2 * i + 2, idx_a, sems.at[0]).start()

    _idx_copy(idx_hbm, 2 * i + 1, idx_b, sems.at[1]).wait()
    process(idx_b, 1)


def _pair_rows(idx_buf, tok, p):
    return idx_buf.at[pl.ds(p * TOK_BLOCK, TOK_BLOCK)][tok]


def _peer_dots_kernel(idx_hbm, x_ref, gate_ref, tab_ref, w_ref, idx_a, idx_b, sems, q_buf, d_buf):
    m4, m2, m1 = _sublane_masks()
    lane = lax.broadcasted_iota(jnp.int32, (SUBLANES, TOK_BLOCK), 1)
    groups = [slice(g * SUBLANES, (g + 1) * SUBLANES) for g in range(PEER_PAIRS // SUBLANES)]

    def process(idx_buf, half):
        base = half * TOK_BLOCK
        q_buf[1] = jnp.zeros((PEER_PAIRS, LANES), jnp.float32)
        d_buf[...] = jnp.zeros((PEER_PAIRS, TOK_BLOCK), jnp.float32)

        def step(t, carry):
            for rows in groups:
                col = jnp.sum(q_buf[(t + 1) & 1, rows, :], axis=1, keepdims=True)
                d_buf[rows, :] = jnp.where(lane == t - 1, col, d_buf[rows, :])

            tok = jnp.minimum(t, TOK_BLOCK - 1)
            xt = x_ref[base + tok]
            xr = pltpu.roll(xt, 4, axis=0)
            xlo = jnp.where(m4, xt, xr)
            xhi = jnp.where(m4, xr, xt)

            def prod(pa):
                wa = tab_ref[pl.ds(_pair_rows(idx_buf, tok, pa), SUBLANES), :]
                wb = tab_ref[pl.ds(_pair_rows(idx_buf, tok, pa + 4), SUBLANES), :]
                lo, hi = _unpack(jnp.where(m4, wa, wb))
                return lo * xlo + hi * xhi

            for rows in groups:
                p = rows.start
                v = [prod(p + a) for a in (0, 2, 1, 3)]
                w = [jnp.where(m2, v1, pltpu.roll(v2, 2, axis=0)) + jnp.where(m2, pltpu.roll(v1, 6, axis=0), v2)
                     for v1, v2 in ((v[0], v[1]), (v[2], v[3]))]
                q_buf[t & 1, rows, :] = (jnp.where(m1, w[0], pltpu.roll(w[1], 1, axis=0))
                                         + jnp.where(m1, pltpu.roll(w[0], 7, axis=0), w[1]))
            return carry

        lax.fori_loop(0, TOK_BLOCK + 1, step, 0)

        dots = d_buf[...].T
        act = 0.5 * dots * (1.0 + lax.erf(dots * (2.0 ** -0.5)))
        out = pl.ds(base, TOK_BLOCK)
        w_ref[out, :] = act * gate_ref[out, :]

    _two_block_step(idx_hbm, idx_a, idx_b, sems, process)


def _peer_scratch():
    return [
        pltpu.SMEM((IDX_BLOCK,), jnp.int32),
        pltpu.SMEM((IDX_BLOCK,), jnp.int32),
        pltpu.SemaphoreType.DMA((2,)),
    ]


def _peer_dots(idx, x, gate, tab):
    T = x.shape[0]
    return pl.pallas_call(
        _peer_dots_kernel,
        grid=(T // (2 * TOK_BLOCK),),
        in_specs=[
            pl.BlockSpec(memory_space=pl.ANY),
            pl.BlockSpec((2 * TOK_BLOCK, SUBLANES, LANES), lambda i: (i, 0, 0)),
            pl.BlockSpec((2 * TOK_BLOCK, PEER_PAIRS), lambda i: (i, 0)),
            pl.BlockSpec(memory_space=pltpu.VMEM),
        ],
        out_specs=pl.BlockSpec((2 * TOK_BLOCK, PEER_PAIRS), lambda i: (i, 0)),
        out_shape=jax.ShapeDtypeStruct((T, PEER_PAIRS), jnp.float32),
        scratch_shapes=_peer_scratch() + [
            pltpu.VMEM((2, PEER_PAIRS, LANES), jnp.float32),
            pltpu.VMEM((PEER_PAIRS, TOK_BLOCK), jnp.float32),
        ],
        compiler_params=pltpu.CompilerParams(
            dimension_semantics=("arbitrary",), vmem_limit_bytes=PEER_VMEM_LIMIT),
        name="peer_dots",
    )(idx.reshape(-1), x.reshape(T, SUBLANES, LANES), gate, tab)


def _peer_mix_kernel(idx_hbm, w_ref, tab_ref, out_ref, idx_a, idx_b, sems, m_buf):
    m4, _, _ = _sublane_masks()
    eye = (lax.broadcasted_iota(jnp.int32, (PEER_PAIRS, PEER_PAIRS), 0)
           == lax.broadcasted_iota(jnp.int32, (PEER_PAIRS, PEER_PAIRS), 1))
    ones = jnp.ones((PEER_PAIRS, LANES), jnp.bfloat16)
    n_acc = 4

    def process(idx_buf, half):
        base = half * TOK_BLOCK

        def spread_weights(tok, slot):
            a = jnp.where(eye, w_ref[pl.ds(base + tok, 1), :], 0.0)
            a_hi = a.astype(jnp.bfloat16)
            a_lo = (a - a_hi.astype(jnp.float32)).astype(jnp.bfloat16)
            m_buf[slot] = (jnp.dot(a_hi, ones, preferred_element_type=jnp.float32)
                           + jnp.dot(a_lo, ones, preferred_element_type=jnp.float32))

        def weight_rows(slot, p):
            return jnp.broadcast_to(m_buf[slot, p:p + 1, :], (SUBLANES, LANES))

        for tok in range(MIX_UNROLL):
            spread_weights(tok, tok)

        def token(t):
            acc_lo = [jnp.zeros((SUBLANES, LANES), jnp.float32) for _ in range(n_acc)]
            acc_hi = [jnp.zeros((SUBLANES, LANES), jnp.float32) for _ in range(n_acc)]
            slot = t & (2 * MIX_UNROLL - 1)
            for j in range(PEER_PAIRS // 2):
                pa = (j // 4) * SUBLANES + j % 4
                wa = tab_ref[pl.ds(_pair_rows(idx_buf, t, pa), SUBLANES), :]
                wb = tab_ref[pl.ds(_pair_rows(idx_buf, t, pa + 4), SUBLANES), :]
                lo, hi = _unpack(jnp.where(m4, wa, wb))
                wt = jnp.where(m4, weight_rows(slot, pa), weight_rows(slot, pa + 4))
                acc_lo[j % n_acc] = acc_lo[j % n_acc] + wt * lo
                acc_hi[j % n_acc] = acc_hi[j % n_acc] + wt * hi
            lo = (acc_lo[0] + acc_lo[1]) + (acc_lo[2] + acc_lo[3])
            hi = (acc_hi[0] + acc_hi[1]) + (acc_hi[2] + acc_hi[3])
            lo = lo + pltpu.roll(lo, 4, axis=0)
            hi = hi + pltpu.roll(hi, 4, axis=0)
            out_ref[base + t] = jnp.where(m4, lo, hi)

        def tokens(k, carry):
            t0 = k * MIX_UNROLL
            for u in range(MIX_UNROLL):
                token(t0 + u)
            for u in range(MIX_UNROLL):
                nxt = jnp.minimum(t0 + MIX_UNROLL + u, TOK_BLOCK - 1)
                spread_weights(nxt, (t0 + MIX_UNROLL + u) & (2 * MIX_UNROLL - 1))
            return carry

        lax.fori_loop(0, TOK_BLOCK // MIX_UNROLL, tokens, 0)

    _two_block_step(idx_hbm, idx_a, idx_b, sems, process)


def _peer_mix(idx, w, tab):
    T = w.shape[0]
    out = pl.pallas_call(
        _peer_mix_kernel,
        grid=(T // (2 * TOK_BLOCK),),
        in_specs=[
            pl.BlockSpec(memory_space=pl.ANY),
            pl.BlockSpec((2 * TOK_BLOCK, PEER_PAIRS), lambda i: (i, 0)),
            pl.BlockSpec(memory_space=pltpu.VMEM),
        ],
        out_specs=pl.BlockSpec((2 * TOK_BLOCK, SUBLANES, LANES), lambda i: (i, 0, 0)),
        out_shape=jax.ShapeDtypeStruct((T, SUBLANES, LANES), jnp.float32),
        scratch_shapes=_peer_scratch() + [pltpu.VMEM((2 * MIX_UNROLL, PEER_PAIRS, LANES), jnp.float32)],
        compiler_params=pltpu.CompilerParams(
            dimension_semantics=("arbitrary",), vmem_limit_bytes=PEER_VMEM_LIMIT),
        name="peer_mix",
    )(idx.reshape(-1), w, tab)
    return out.reshape(T, D_MODEL)


def _peer(h, wq, subkeys, u_tab, v_tab):
    B, S, D = h.shape
    hf = h.reshape(B * S, D)
    wqt = wq.T.astype(jnp.bfloat16)
    keys = subkeys.reshape(PEER_HEADS * 2, N_KEYS, PEER_HALF).astype(jnp.bfloat16)
    idx, gate = _peer_router(hf, wqt, keys)
    w = _peer_dots(idx, hf, gate, _pack_table(u_tab))
    return _peer_mix(idx, w, _pack_table(v_tab)).reshape(B, S, D)


def _final_norm_kernel(x_ref, f_ref, g2_ref, gain_ref, bias_ref, o_ref):
    y = DEEPNORM_ALPHA * x_ref[0] + g2_ref[0] * f_ref[0]
    mu = jnp.mean(y, axis=-1, keepdims=True)
    var = jnp.mean(jnp.square(y - mu), axis=-1, keepdims=True)
    o_ref[0] = (y - mu) * lax.rsqrt(var + LN_EPS) * gain_ref[...] + bias_ref[...]


def _final_norm(x, ffn, g2, gain, bias):
    B, S, D = x.shape
    ts = 512
    return pl.pallas_call(
        _final_norm_kernel,
        grid=(B, S // ts),
        in_specs=[
            pl.BlockSpec((1, ts, D), lambda b, s: (b, s, 0)),
            pl.BlockSpec((1, ts, D), lambda b, s: (b, s, 0)),
            pl.BlockSpec((1, 1, D), lambda b, s: (b, 0, 0)),
            pl.BlockSpec((1, D), lambda b, s: (0, 0)),
            pl.BlockSpec((1, D), lambda b, s: (0, 0)),
        ],
        out_specs=pl.BlockSpec((1, ts, D), lambda b, s: (b, s, 0)),
        out_shape=jax.ShapeDtypeStruct((B, S, D), jnp.float32),
        name="final_norm",
    )(x, ffn, g2[:, None, :], gain[None, :], bias[None, :])


def kernel(x, c, positions, w_ada, b_ada, w_in, conv_dw, conv_dw_b, conv_ln_g, conv_ln_b, w_conv_out, b_conv_out, w_out, ln1_g, ln1_b, peer_wq, peer_subkeys, peer_u, peer_v, ln2_g, ln2_b):
    cond = jax.nn.silu(c)
    l = 0
    mod = cond @ w_ada[l] + b_ada[l]
    sh1, sc1, g1, sh2, sc2, g2 = jnp.split(mod, 6, axis=-1)
    h1 = _layer_norm(x) * (1.0 + sc1[:, None, :]) + sh1[:, None, :]
    mix = _mixer(h1, positions, w_in[l], conv_dw[l], conv_dw_b[l], conv_ln_g[l],
                 conv_ln_b[l], w_conv_out[l], b_conv_out[l], w_out[l])
    x = _layer_norm(DEEPNORM_ALPHA * x + g1[:, None, :] * mix, ln1_g[l], ln1_b[l])
    h2 = _layer_norm(x) * (1.0 + sc2[:, None, :]) + sh2[:, None, :]
    ffn = _peer(h2, peer_wq[l], peer_subkeys[l], peer_u[l], peer_v[l])
    return _final_norm(x, ffn, g2, ln2_g[l], ln2_b[l])
```

```python
import jax
import jax.numpy as jnp
from jax import lax
from jax.experimental import pallas as pl
from jax.experimental.pallas import tpu as pltpu

D_MODEL = 1024
CHUNK = 64
RET_HEADS = 4
RET_HEAD_DIM = 256
RET_WIDTH = RET_HEADS * RET_HEAD_DIM
ROPE_THETA = 10000.0
CONV_WIDTH = D_MODEL
CONV_TAPS = 31
PEER_HEADS = 8
PEER_HALF = 128
N_KEYS = 128
N_EXPERTS = N_KEYS * N_KEYS
PEER_TOPK = 16
PEER_PAIRS = PEER_HEADS * PEER_TOPK
DEPTH = 1
DEEPNORM_ALPHA = (2.0 * DEPTH) ** 0.25
LN_EPS = 1e-5

SUBLANES = 8
LANES = 128
ROWS_PER_EXPERT = D_MODEL // (2 * LANES)
TAB_FRONT = SUBLANES
HI_MASK = -65536
TOK_BLOCK = 128
IDX_BLOCK = PEER_PAIRS * TOK_BLOCK
MIX_UNROLL = 2
PEER_VMEM_LIMIT = 52 * 1024 * 1024
PROJ_TM = 512
PROJ_TN = 2048
RET_BATCH = 4
MERGE_TS = 256
CONV_HALO = 32
CONV_BLOCKS = 4


def _adaln_kernel(c_ref, w_ref, b_ref, o_ref):
    c = c_ref[...]
    cond = (c * jax.nn.sigmoid(c)).astype(jnp.bfloat16)
    o_ref[...] = jnp.dot(cond, w_ref[...].astype(jnp.bfloat16),
                         preferred_element_type=jnp.float32) + b_ref[...]


def _adaln(c, w, b):
    B = c.shape[0]
    n = w.shape[1]
    return pl.pallas_call(
        _adaln_kernel,
        grid=(n // D_MODEL,),
        in_specs=[
            pl.BlockSpec((B, D_MODEL), lambda j: (0, 0)),
            pl.BlockSpec((D_MODEL, D_MODEL), lambda j: (0, j)),
            pl.BlockSpec((1, D_MODEL), lambda j: (0, j)),
        ],
        out_specs=pl.BlockSpec((B, D_MODEL), lambda j: (0, j)),
        out_shape=jax.ShapeDtypeStruct((B, n), jnp.float32),
        name="adaln",
    )(c, w, b[None, :])


def _norm_rows(x):
    mu = jnp.mean(x, axis=-1, keepdims=True)
    xc = x - mu
    var = jnp.mean(xc * xc, axis=-1, keepdims=True)
    return xc * lax.rsqrt(var + LN_EPS)


def _proj_kernel(x_ref, sc_ref, sh_ref, w_ref, o_ref, h_ref):
    @pl.when(pl.program_id(2) == 0)
    def _():
        h_ref[...] = (_norm_rows(x_ref[0]) * (1.0 + sc_ref[0]) + sh_ref[0]).astype(jnp.bfloat16)

    o_ref[0] = jnp.dot(h_ref[...], w_ref[...], preferred_element_type=jnp.float32).astype(o_ref.dtype)


def _input_proj(x, sc, sh, w):
    B, S, D = x.shape
    n = w.shape[1]
    return pl.pallas_call(
        _proj_kernel,
        grid=(B, S // PROJ_TM, n // PROJ_TN),
        in_specs=[
            pl.BlockSpec((1, PROJ_TM, D), lambda b, i, j: (b, i, 0)),
            pl.BlockSpec((1, 1, D), lambda b, i, j: (b, 0, 0)),
            pl.BlockSpec((1, 1, D), lambda b, i, j: (b, 0, 0)),
            pl.BlockSpec((D, PROJ_TN), lambda b, i, j: (0, j)),
        ],
        out_specs=pl.BlockSpec((1, PROJ_TM, PROJ_TN), lambda b, i, j: (b, i, j)),
        out_shape=jax.ShapeDtypeStruct((B, S, n), jnp.bfloat16),
        scratch_shapes=[pltpu.VMEM((PROJ_TM, D), jnp.bfloat16)],
        compiler_params=pltpu.CompilerParams(
            dimension_semantics=("arbitrary", "arbitrary", "arbitrary")),
        name="input_proj",
    )(x, sc[:, None, :], sh[:, None, :], w)


def _retention_consts():
    h = jnp.arange(RET_HEADS, dtype=jnp.float32)
    log_gamma = jnp.log(1.0 - 2.0 ** (-5.0 - h))
    pos = jnp.arange(CHUNK, dtype=jnp.float32)
    rel = jnp.abs(pos[:, None] - pos[None, :])
    intra = jnp.exp(log_gamma[:, None, None] * rel)
    q_decay = jnp.exp(log_gamma[:, None, None] * (pos[None, :, None] + 1.0))
    k_decay = jnp.exp(log_gamma[:, None, None] * (CHUNK - 1.0 - pos[None, :, None]))
    chunk_decay = jnp.broadcast_to(jnp.exp(log_gamma * CHUNK)[:, None, None], (RET_HEADS, 1, LANES))
    half = RET_HEAD_DIM // 2
    inv_freq = ROPE_THETA ** (-jnp.arange(half, dtype=jnp.float32) / half)
    return inv_freq[None, :], intra, q_decay, k_decay, chunk_decay


def _retention_kernel(q_ref, k_ref, v_ref, g_ref, pos_ref, invf_ref, intra_ref, qd_ref, kd_ref, cd_ref,
                      o_ref, state_ref):
    @pl.when(pl.program_id(1) == 0)
    def _():
        state_ref[...] = jnp.zeros(state_ref.shape, jnp.float32)

    half = RET_HEAD_DIM // 2

    def rotate(t, cos, sin):
        t1, t2 = t[:, :half], t[:, half:]
        return jnp.concatenate([t1 * cos - t2 * sin, t1 * sin + t2 * cos], axis=1)

    for b in range(RET_BATCH):
        ang = pos_ref[b] * invf_ref[...]
        cos, sin = jnp.cos(ang), jnp.sin(ang)
        for h in range(RET_HEADS):
            cols = slice(h * RET_HEAD_DIM, (h + 1) * RET_HEAD_DIM)
            q = rotate(q_ref[b, :, cols].astype(jnp.float32), cos, sin).astype(jnp.bfloat16)
            k = rotate(k_ref[b, :, cols].astype(jnp.float32), cos, sin) * (RET_HEAD_DIM ** -0.5)
            v = v_ref[b, :, cols]
            s = lax.dot_general(q, k.astype(jnp.bfloat16), (((1,), (1,)), ((), ())),
                                preferred_element_type=jnp.float32) * intra_ref[h]
            intra = jnp.dot(s.astype(jnp.bfloat16), v, preferred_element_type=jnp.float32)
            state = state_ref[b, h]
            cross = jnp.dot(q, state.astype(jnp.bfloat16), preferred_element_type=jnp.float32) * qd_ref[h]
            kd = (k * kd_ref[h]).astype(jnp.bfloat16)
            state_ref[b, h] = state * cd_ref[h][:, :1] + lax.dot_general(
                kd, v, (((0,), (0,)), ((), ())), preferred_element_type=jnp.float32)
            g = g_ref[b, :, cols].astype(jnp.float32)
            o_ref[b, :, cols] = _norm_rows(intra + cross) * (g * jax.nn.sigmoid(g))


def _retention(proj, pos, consts):
    B, S, _ = proj.shape
    blk = (RET_BATCH, CHUNK, RET_WIDTH)

    def col(j):
        return pl.BlockSpec(blk, lambda b, c: (b, c, j))

    def const(a):
        return pl.BlockSpec(a.shape, lambda b, c: (0,) * a.ndim)

    return pl.pallas_call(
        _retention_kernel,
        grid=(B // RET_BATCH, S // CHUNK),
        in_specs=[col(0), col(1), col(2), col(3),
                  pl.BlockSpec((RET_BATCH, CHUNK, 1), lambda b, c: (b, c, 0))] + [const(a) for a in consts],
        out_specs=pl.BlockSpec(blk, lambda b, c: (b, c, 0)),
        out_shape=jax.ShapeDtypeStruct((B, S, RET_WIDTH), jnp.float32),
        scratch_shapes=[pltpu.VMEM((RET_BATCH, RET_HEADS, RET_HEAD_DIM, RET_HEAD_DIM), jnp.float32)],
        compiler_params=pltpu.CompilerParams(dimension_semantics=("arbitrary", "arbitrary")),
        name="retention",
    )(proj, proj, proj, proj, pos, *consts)


def _merge_kernel(cin_ref, halo_ref, ga_ref, gb_ref, ret_ref, x_ref, dw_ref, dwb_ref, cg_ref, cb_ref,
                  wc_ref, bc_ref, wo_ref, g1_ref, sc2_ref, sh2_ref, l1g_ref, l1b_ref,
                  x1_ref, h2_ref, y_buf, c_buf):
    def glu(a):
        a = a.astype(jnp.float32)
        return a[:, :CONV_WIDTH] * jax.nn.sigmoid(a[:, CONV_WIDTH:])

    first = pl.program_id(1) == 0
    halo = jnp.where(first, 0.0, glu(halo_ref[0]))
    main = glu(cin_ref[0])
    n_tiles = CONV_WIDTH // LANES
    for lt in range(n_tiles):
        lanes = slice(lt * LANES, (lt + 1) * LANES)
        y_buf[lt, 0:CONV_HALO, :] = halo[:, lanes]
        y_buf[lt, CONV_HALO:CONV_HALO + MERGE_TS, :] = main[:, lanes]

    shift = CONV_HALO - (CONV_TAPS - 1)

    def conv_rows(r, carry):
        row = r * (CONV_BLOCKS * SUBLANES)
        for lt in range(n_tiles):
            acc = [jnp.zeros((SUBLANES, LANES), jnp.float32) for _ in range(CONV_BLOCKS)]
            for j in range(CONV_TAPS):
                tap = jnp.broadcast_to(dw_ref[j, lt:lt + 1, :], (SUBLANES, LANES))
                for i in range(CONV_BLOCKS):
                    acc[i] = acc[i] + tap * y_buf[lt, pl.ds(row + i * SUBLANES + shift + j, SUBLANES), :]
            for i in range(CONV_BLOCKS):
                c_buf[lt, pl.ds(row + i * SUBLANES, SUBLANES), :] = acc[i]
        return carry

    lax.fori_loop(0, MERGE_TS // (CONV_BLOCKS * SUBLANES), conv_rows, 0)

    conv_dw_out = jnp.concatenate([c_buf[lt] for lt in range(n_tiles)], axis=1) + dwb_ref[...]
    c = _norm_rows(conv_dw_out) * cg_ref[...] + cb_ref[...]
    c = (c * jax.nn.sigmoid(c)).astype(jnp.bfloat16)
    conv = jnp.dot(c, wc_ref[...], preferred_element_type=jnp.float32) + bc_ref[...]
    merged = (jax.nn.sigmoid(ga_ref[0].astype(jnp.float32)) * ret_ref[0]
              + jax.nn.sigmoid(gb_ref[0].astype(jnp.float32)) * conv)
    mix = jnp.dot(merged.astype(jnp.bfloat16), wo_ref[...], preferred_element_type=jnp.float32)
    x1 = _norm_rows(DEEPNORM_ALPHA * x_ref[0] + g1_ref[0] * mix) * l1g_ref[...] + l1b_ref[...]
    x1_ref[0] = x1
    h2_ref[0] = _norm_rows(x1) * (1.0 + sc2_ref[0]) + sh2_ref[0]


def _merge(proj, ret, x, conv_dw, conv_dw_b, conv_ln_g, conv_ln_b, w_conv_out, b_conv_out, w_out,
           g1, sc2, sh2, ln1_g, ln1_b):
    B, S, D = x.shape
    glu_col = 4 * RET_WIDTH // (2 * CONV_WIDTH)
    gate_col = (4 * RET_WIDTH + 2 * CONV_WIDTH) // D
    halo_per_tile = MERGE_TS // CONV_HALO

    def tile(width, j):
        return pl.BlockSpec((1, MERGE_TS, width), lambda b, s: (b, s, j))

    def row(a):
        return pl.BlockSpec((1, a.shape[-1]), lambda b, s: (0, 0))

    def per_batch():
        return pl.BlockSpec((1, 1, D), lambda b, s: (b, 0, 0))

    def whole(a):
        return pl.BlockSpec(a.shape, lambda b, s: (0,) * a.ndim)

    rows = [conv_dw_b[None, :], conv_ln_g[None, :], conv_ln_b[None, :]]
    wc = w_conv_out.astype(jnp.bfloat16)
    dw3 = conv_dw.reshape(CONV_TAPS, CONV_WIDTH // LANES, LANES)
    wo = w_out.astype(jnp.bfloat16)
    out_spec = pl.BlockSpec((1, MERGE_TS, D), lambda b, s: (b, s, 0))
    return pl.pallas_call(
        _merge_kernel,
        grid=(B, S // MERGE_TS),
        in_specs=[
            tile(2 * CONV_WIDTH, glu_col),
            pl.BlockSpec((1, CONV_HALO, 2 * CONV_WIDTH),
                         lambda b, s: (b, jnp.maximum(s * halo_per_tile - 1, 0), glu_col)),
            tile(D, gate_col), tile(D, gate_col + 1), tile(D, 0), tile(D, 0),
            whole(dw3), row(rows[0]), row(rows[1]), row(rows[2]),
            whole(wc), row(b_conv_out[None, :]), whole(wo),
            per_batch(), per_batch(), per_batch(), row(ln1_g[None, :]), row(ln1_b[None, :]),
        ],
        out_specs=[out_spec, out_spec],
        out_shape=[jax.ShapeDtypeStruct((B, S, D), jnp.float32)] * 2,
        scratch_shapes=[
            pltpu.VMEM((CONV_WIDTH // LANES, CONV_HALO + MERGE_TS, LANES), jnp.float32),
            pltpu.VMEM((CONV_WIDTH // LANES, MERGE_TS, LANES), jnp.float32),
        ],
        compiler_params=pltpu.CompilerParams(dimension_semantics=("arbitrary", "arbitrary")),
        name="merge",
    )(proj, proj, proj, proj, ret, x, dw3, rows[0], rows[1], rows[2], wc, b_conv_out[None, :], wo,
      g1[:, None, :], sc2[:, None, :], sh2[:, None, :], ln1_g[None, :], ln1_b[None, :])


def _topk_rows(s, k):
    n_rows = s.shape[0]
    row = lax.broadcasted_iota(jnp.int32, s.shape, 0)
    vals, idxs = [], []
    for _ in range(k):
        m = jnp.max(s, axis=0, keepdims=True)
        i = jnp.min(jnp.where(s == m, row, n_rows), axis=0, keepdims=True)
        vals.append(m)
        idxs.append(i)
        s = jnp.where(row == i, -jnp.inf, s)
    return jnp.concatenate(vals, axis=0), jnp.concatenate(idxs, axis=0)


def _candidate_tiles():
    k = PEER_TOPK
    tiles = []
    for j in range(SUBLANES):
        n_i = k // (j + 1)
        for i0 in range(0, n_i, SUBLANES):
            tiles.append((i0, min(SUBLANES, n_i - i0), j, None))
    tiles.append((0, 1, None, SUBLANES))
    return tiles


def _router_kernel(h_ref, wqt_ref, keys_ref, idx_ref, gate_ref):
    k = PEER_TOPK
    hb = h_ref[...].astype(jnp.bfloat16)
    qt = lax.dot_general(wqt_ref[...], hb, (((1,), (1,)), ((), ())),
                         preferred_element_type=jnp.float32)
    sub = lax.broadcasted_iota(jnp.int32, (SUBLANES, TOK_BLOCK), 0)
    tiles = _candidate_tiles()
    tile_pos = []
    for i0, n_i, j, j0 in tiles:
        if j is not None:
            tile_pos.append(jnp.where(sub < n_i, (sub + i0) * k + j, k * k))
        else:
            tile_pos.append(sub + j0)
    all_e, all_gate = [], []
    for h in range(PEER_HEADS):
        tops = []
        for p in range(2):
            hp = 2 * h + p
            qhp = qt[hp * PEER_HALF:(hp + 1) * PEER_HALF, :].astype(jnp.bfloat16)
            s = jnp.dot(keys_ref[hp], qhp, preferred_element_type=jnp.float32)
            tops.append(_topk_rows(s, k))
        (s0, i0v), (s1, i1v) = tops
        e0 = i0v * N_KEYS
        cand_s, cand_e = [], []
        for (i0, n_i, j, j0), pos in zip(tiles, tile_pos):
            if j is not None:
                cs = s0[i0:i0 + SUBLANES, :] + s1[j:j + 1, :]
                ce = e0[i0:i0 + SUBLANES, :] + i1v[j:j + 1, :]
                cand_s.append(jnp.where(pos < k * k, cs, -jnp.inf))
            else:
                cs = s0[0:1, :] + s1[j0:j0 + SUBLANES, :]
                ce = e0[0:1, :] + i1v[j0:j0 + SUBLANES, :]
                cand_s.append(cs)
            cand_e.append(ce)
        cand_s = jnp.concatenate(cand_s, axis=0)
        cand_e = jnp.concatenate(cand_e, axis=0)
        cand_pos = jnp.concatenate(tile_pos, axis=0)
        best_s, best_e = [], []
        for _ in range(k):
            m = jnp.max(cand_s, axis=0, keepdims=True)
            pos = jnp.min(jnp.where(cand_s == m, cand_pos, k * k), axis=0, keepdims=True)
            hit = cand_pos == pos
            best_s.append(m)
            best_e.append(jnp.max(jnp.where(hit, cand_e, -1), axis=0, keepdims=True))
            cand_s = jnp.where(hit, -jnp.inf, cand_s)
        best_s = jnp.concatenate(best_s, axis=0)
        ex = jnp.exp(best_s - best_s[0:1, :])
        all_gate.append(ex / jnp.sum(ex, axis=0, keepdims=True))
        all_e.append(jnp.concatenate(best_e, axis=0))
    pair = lax.broadcasted_iota(jnp.int32, (PEER_PAIRS, TOK_BLOCK), 0)
    first = (pair & 4) == 0
    idx_ref[...] = (jnp.concatenate(all_e, axis=0) * ROWS_PER_EXPERT
                    + jnp.where(first, TAB_FRONT, TAB_FRONT - ROWS_PER_EXPERT))
    gate_ref[...] = jnp.concatenate(all_gate, axis=0).T


def _peer_router(h, wqt, keys):
    T = h.shape[0]
    return pl.pallas_call(
        _router_kernel,
        grid=(T // TOK_BLOCK,),
        in_specs=[
            pl.BlockSpec((TOK_BLOCK, D_MODEL), lambda i: (i, 0)),
            pl.BlockSpec(wqt.shape, lambda i: (0, 0)),
            pl.BlockSpec(keys.shape, lambda i: (0, 0, 0)),
        ],
        out_specs=[
            pl.BlockSpec((PEER_PAIRS, TOK_BLOCK), lambda i: (i, 0)),
            pl.BlockSpec((TOK_BLOCK, PEER_PAIRS), lambda i: (i, 0)),
        ],
        out_shape=[
            jax.ShapeDtypeStruct((T // TOK_BLOCK * PEER_PAIRS, TOK_BLOCK), jnp.int32),
            jax.ShapeDtypeStruct((T, PEER_PAIRS), jnp.float32),
        ],
        compiler_params=pltpu.CompilerParams(dimension_semantics=("arbitrary",)),
        name="peer_router",
    )(h, wqt, keys)


def _pack_table(tab):
    n = tab.shape[0]
    b = lax.bitcast_convert_type(tab.astype(jnp.bfloat16), jnp.uint16).astype(jnp.uint32)
    b = b.reshape(n, 2, ROWS_PER_EXPERT, LANES)
    packed = lax.bitcast_convert_type(b[:, 0] | (b[:, 1] << 16), jnp.int32)
    packed = packed.reshape(n * ROWS_PER_EXPERT, LANES)
    pad = jnp.zeros((TAB_FRONT, LANES), jnp.int32)
    return jnp.concatenate([pad, packed, pad], axis=0)


def _unpack(c):
    lo = lax.bitcast_convert_type(c << 16, jnp.float32)
    hi = lax.bitcast_convert_type(c & HI_MASK, jnp.float32)
    return lo, hi


def _sublane_masks():
    sub = lax.broadcasted_iota(jnp.int32, (SUBLANES, LANES), 0)
    return sub < 4, (sub & 2) == 0, (sub & 1) == 0


def _idx_copy(idx_hbm, block, buf, sem):
    return pltpu.make_async_copy(idx_hbm.at[pl.ds(block * IDX_BLOCK, IDX_BLOCK)], buf, sem)


def _two_block_step(idx_hbm, idx_a, idx_b, sems, process):
    i = pl.program_id(0)

    @pl.when(i == 0)
    def _():
        _idx_copy(idx_hbm, 0, idx_a, sems.at[0]).start()

    _idx_copy(idx_hbm, 2 * i + 1, idx_b, sems.at[1]).start()
    _idx_copy(idx_hbm, 2 * i, idx_a, sems.at[0]).wait()
    process(idx_a, 0)

    @pl.when(i + 1 < pl.num_programs(0))
    def _():
        _idx_copy(idx_hbm, 2 * i + 2, idx_a, sems.at[0]).start()

    _idx_copy(idx_hbm, 2 * i + 1, idx_b, sems.at[1]).wait()
    process(idx_b, 1)


def _pair_rows(idx_buf, tok, p):
    return idx_buf.at[pl.ds(p * TOK_BLOCK, TOK_BLOCK)][tok]


def _peer_dots_kernel(idx_hbm, x_ref, gate_ref, tab_ref, w_ref, idx_a, idx_b, sems, q_buf, d_buf):
    m4, m2, m1 = _sublane_masks()
    lane = lax.broadcasted_iota(jnp.int32, (SUBLANES, TOK_BLOCK), 1)
    groups = [slice(g * SUBLANES, (g + 1) * SUBLANES) for g in range(PEER_PAIRS // SUBLANES)]

    def process(idx_buf, half):
        base = half * TOK_BLOCK
        q_buf[1] = jnp.zeros((PEER_PAIRS, LANES), jnp.float32)
        d_buf[...] = jnp.zeros((PEER_PAIRS, TOK_BLOCK), jnp.float32)

        def step(t, carry):
            for rows in groups:
                col = jnp.sum(q_buf[(t + 1) & 1, rows, :], axis=1, keepdims=True)
                d_buf[rows, :] = jnp.where(lane == t - 1, col, d_buf[rows, :])

            tok = jnp.minimum(t, TOK_BLOCK - 1)
            xt = x_ref[base + tok]
            xr = pltpu.roll(xt, 4, axis=0)
            xlo = jnp.where(m4, xt, xr)
            xhi = jnp.where(m4, xr, xt)

            def prod(pa):
                wa = tab_ref[pl.ds(_pair_rows(idx_buf, tok, pa), SUBLANES), :]
                wb = tab_ref[pl.ds(_pair_rows(idx_buf, tok, pa + 4), SUBLANES), :]
                lo, hi = _unpack(jnp.where(m4, wa, wb))
                return lo * xlo + hi * xhi

            for rows in groups:
                p = rows.start
                v = [prod(p + a) for a in (0, 2, 1, 3)]
                w = [jnp.where(m2, v1, pltpu.roll(v2, 2, axis=0)) + jnp.where(m2, pltpu.roll(v1, 6, axis=0), v2)
                     for v1, v2 in ((v[0], v[1]), (v[2], v[3]))]
                q_buf[t & 1, rows, :] = (jnp.where(m1, w[0], pltpu.roll(w[1], 1, axis=0))
                                         + jnp.where(m1, pltpu.roll(w[0], 7, axis=0), w[1]))
            return carry

        lax.fori_loop(0, TOK_BLOCK + 1, step, 0)

        dots = d_buf[...].T
        act = 0.5 * dots * (1.0 + lax.erf(dots * (2.0 ** -0.5)))
        out = pl.ds(base, TOK_BLOCK)
        w_ref[out, :] = act * gate_ref[out, :]

    _two_block_step(idx_hbm, idx_a, idx_b, sems, process)


def _peer_scratch():
    return [
        pltpu.SMEM((IDX_BLOCK,), jnp.int32),
        pltpu.SMEM((IDX_BLOCK,), jnp.int32),
        pltpu.SemaphoreType.DMA((2,)),
    ]


def _peer_dots(idx, x, gate, tab):
    T = x.shape[0]
    return pl.pallas_call(
        _peer_dots_kernel,
        grid=(T // (2 * TOK_BLOCK),),
        in_specs=[
            pl.BlockSpec(memory_space=pl.ANY),
            pl.BlockSpec((2 * TOK_BLOCK, SUBLANES, LANES), lambda i: (i, 0, 0)),
            pl.BlockSpec((2 * TOK_BLOCK, PEER_PAIRS), lambda i: (i, 0)),
            pl.BlockSpec(memory_space=pltpu.VMEM),
        ],
        out_specs=pl.BlockSpec((2 * TOK_BLOCK, PEER_PAIRS), lambda i: (i, 0)),
        out_shape=jax.ShapeDtypeStruct((T, PEER_PAIRS), jnp.float32),
        scratch_shapes=_peer_scratch() + [
            pltpu.VMEM((2, PEER_PAIRS, LANES), jnp.float32),
            pltpu.VMEM((PEER_PAIRS, TOK_BLOCK), jnp.float32),
        ],
        compiler_params=pltpu.CompilerParams(
            dimension_semantics=("arbitrary",), vmem_limit_bytes=PEER_VMEM_LIMIT),
        name="peer_dots",
    )(idx.reshape(-1), x.reshape(T, SUBLANES, LANES), gate, tab)


def _peer_mix_kernel(idx_hbm, w_ref, tab_ref, out_ref, idx_a, idx_b, sems, m_buf):
    m4, _, _ = _sublane_masks()
    eye = (lax.broadcasted_iota(jnp.int32, (PEER_PAIRS, PEER_PAIRS), 0)
           == lax.broadcasted_iota(jnp.int32, (PEER_PAIRS, PEER_PAIRS), 1))
    ones = jnp.ones((PEER_PAIRS, LANES), jnp.bfloat16)
    n_acc = 4

    def process(idx_buf, half):
        base = half * TOK_BLOCK

        def spread_weights(tok, slot):
            a = jnp.where(eye, w_ref[pl.ds(base + tok, 1), :], 0.0)
            a_hi = a.astype(jnp.bfloat16)
            a_lo = (a - a_hi.astype(jnp.float32)).astype(jnp.bfloat16)
            m_buf[slot] = (jnp.dot(a_hi, ones, preferred_element_type=jnp.float32)
                           + jnp.dot(a_lo, ones, preferred_element_type=jnp.float32))

        def weight_rows(slot, p):
            return jnp.broadcast_to(m_buf[slot, p:p + 1, :], (SUBLANES, LANES))

        for tok in range(MIX_UNROLL):
            spread_weights(tok, tok)

        def token(t):
            acc_lo = [jnp.zeros((SUBLANES, LANES), jnp.float32) for _ in range(n_acc)]
            acc_hi = [jnp.zeros((SUBLANES, LANES), jnp.float32) for _ in range(n_acc)]
            slot = t & (2 * MIX_UNROLL - 1)
            for j in range(PEER_PAIRS // 2):
                pa = (j // 4) * SUBLANES + j % 4
                wa = tab_ref[pl.ds(_pair_rows(idx_buf, t, pa), SUBLANES), :]
                wb = tab_ref[pl.ds(_pair_rows(idx_buf, t, pa + 4), SUBLANES), :]
                lo, hi = _unpack(jnp.where(m4, wa, wb))
                wt = jnp.where(m4, weight_rows(slot, pa), weight_rows(slot, pa + 4))
                acc_lo[j % n_acc] = acc_lo[j % n_acc] + wt * lo
                acc_hi[j % n_acc] = acc_hi[j % n_acc] + wt * hi
            lo = (acc_lo[0] + acc_lo[1]) + (acc_lo[2] + acc_lo[3])
            hi = (acc_hi[0] + acc_hi[1]) + (acc_hi[2] + acc_hi[3])
            lo = lo + pltpu.roll(lo, 4, axis=0)
            hi = hi + pltpu.roll(hi, 4, axis=0)
            out_ref[base + t] = jnp.where(m4, lo, hi)

        def tokens(k, carry):
            t0 = k * MIX_UNROLL
            for u in range(MIX_UNROLL):
                token(t0 + u)
            for u in range(MIX_UNROLL):
                nxt = jnp.minimum(t0 + MIX_UNROLL + u, TOK_BLOCK - 1)
                spread_weights(nxt, (t0 + MIX_UNROLL + u) & (2 * MIX_UNROLL - 1))
            return carry

        lax.fori_loop(0, TOK_BLOCK // MIX_UNROLL, tokens, 0)

    _two_block_step(idx_hbm, idx_a, idx_b, sems, process)


def _peer_mix(idx, w, tab):
    T = w.shape[0]
    out = pl.pallas_call(
        _peer_mix_kernel,
        grid=(T // (2 * TOK_BLOCK),),
        in_specs=[
            pl.BlockSpec(memory_space=pl.ANY),
            pl.BlockSpec((2 * TOK_BLOCK, PEER_PAIRS), lambda i: (i, 0)),
            pl.BlockSpec(memory_space=pltpu.VMEM),
        ],
        out_specs=pl.BlockSpec((2 * TOK_BLOCK, SUBLANES, LANES), lambda i: (i, 0, 0)),
        out_shape=jax.ShapeDtypeStruct((T, SUBLANES, LANES), jnp.float32),
        scratch_shapes=_peer_scratch() + [pltpu.VMEM((2 * MIX_UNROLL, PEER_PAIRS, LANES), jnp.float32)],
        compiler_params=pltpu.CompilerParams(
            dimension_semantics=("arbitrary",), vmem_limit_bytes=PEER_VMEM_LIMIT),
        name="peer_mix",
    )(idx.reshape(-1), w, tab)
    return out.reshape(T, D_MODEL)


def _peer(h, wq, subkeys, u_tab, v_tab):
    B, S, D = h.shape
    hf = h.reshape(B * S, D)
    wqt = wq.T.astype(jnp.bfloat16)
    keys = subkeys.reshape(PEER_HEADS * 2, N_KEYS, PEER_HALF).astype(jnp.bfloat16)
    idx, gate = _peer_router(hf, wqt, keys)
    w = _peer_dots(idx, hf, gate, _pack_table(u_tab))
    return _peer_mix(idx, w, _pack_table(v_tab)).reshape(B, S, D)


def _final_norm_kernel(x_ref, f_ref, g2_ref, gain_ref, bias_ref, o_ref):
    y = DEEPNORM_ALPHA * x_ref[0] + g2_ref[0] * f_ref[0]
    mu = jnp.mean(y, axis=-1, keepdims=True)
    var = jnp.mean(jnp.square(y - mu), axis=-1, keepdims=True)
    o_ref[0] = (y - mu) * lax.rsqrt(var + LN_EPS) * gain_ref[...] + bias_ref[...]


def _final_norm(x, ffn, g2, gain, bias):
    B, S, D = x.shape
    ts = 512
    return pl.pallas_call(
        _final_norm_kernel,
        grid=(B, S // ts),
        in_specs=[
            pl.BlockSpec((1, ts, D), lambda b, s: (b, s, 0)),
            pl.BlockSpec((1, ts, D), lambda b, s: (b, s, 0)),
            pl.BlockSpec((1, 1, D), lambda b, s: (b, 0, 0)),
            pl.BlockSpec((1, D), lambda b, s: (0, 0)),
            pl.BlockSpec((1, D), lambda b, s: (0, 0)),
        ],
        out_specs=pl.BlockSpec((1, ts, D), lambda b, s: (b, s, 0)),
        out_shape=jax.ShapeDtypeStruct((B, S, D), jnp.float32),
        name="final_norm",
    )(x, ffn, g2[:, None, :], gain[None, :], bias[None, :])


def kernel(x, c, positions, w_ada, b_ada, w_in, conv_dw, conv_dw_b, conv_ln_g, conv_ln_b, w_conv_out, b_conv_out, w_out, ln1_g, ln1_b, peer_wq, peer_subkeys, peer_u, peer_v, ln2_g, ln2_b):
    l = 0
    mod = _adaln(c, w_ada[l], b_ada[l])
    sh1, sc1, g1, sh2, sc2, g2 = jnp.split(mod, 6, axis=-1)
    proj = _input_proj(x, sc1, sh1, w_in[l].astype(jnp.bfloat16))
    ret = _retention(proj, positions.astype(jnp.float32)[..., None], _retention_consts())
    x1, h2 = _merge(proj, ret, x, conv_dw[l], conv_dw_b[l], conv_ln_g[l], conv_ln_b[l],
                    w_conv_out[l], b_conv_out[l], w_out[l], g1, sc2, sh2, ln1_g[l], ln1_b[l])
    ffn = _peer(h2, peer_wq[l], peer_subkeys[l], peer_u[l], peer_v[l])
    return _final_norm(x1, ffn, g2, ln2_g[l], ln2_b[l])
```

```python
import jax
import jax.numpy as jnp
from jax import lax
from jax.experimental import pallas as pl
from jax.experimental.pallas import tpu as pltpu

D_MODEL = 1024
CHUNK = 64
RET_HEADS = 4
RET_HEAD_DIM = 256
RET_WIDTH = RET_HEADS * RET_HEAD_DIM
ROPE_THETA = 10000.0
CONV_WIDTH = D_MODEL
CONV_TAPS = 31
PEER_HEADS = 8
PEER_HALF = 128
N_KEYS = 128
N_EXPERTS = N_KEYS * N_KEYS
PEER_TOPK = 16
PEER_PAIRS = PEER_HEADS * PEER_TOPK
DEPTH = 1
DEEPNORM_ALPHA = (2.0 * DEPTH) ** 0.25
LN_EPS = 1e-5

SUBLANES = 8
LANES = 128
ROWS_PER_EXPERT = D_MODEL // (2 * LANES)
PACK_ROWS = 256
TAB_FRONT = PACK_ROWS * ROWS_PER_EXPERT
HI_MASK = -65536
TOK_BLOCK = 128
IDX_BLOCK = PEER_PAIRS * TOK_BLOCK
MIX_UNROLL = 1
PEER_VMEM_LIMIT = 52 * 1024 * 1024
PROJ_TM = 512
PROJ_TN = 2048
RET_BATCH = 4
MERGE_TS = 256
CONV_HALO = 32
CONV_BLOCKS = 4


def _adaln_kernel(c_ref, w_ref, b_ref, o_ref):
    c = c_ref[...]
    cond = (c * jax.nn.sigmoid(c)).astype(jnp.bfloat16)
    o_ref[...] = jnp.dot(cond, w_ref[...].astype(jnp.bfloat16),
                         preferred_element_type=jnp.float32) + b_ref[...]


def _adaln(c, w, b):
    B = c.shape[0]
    n = w.shape[1]
    return pl.pallas_call(
        _adaln_kernel,
        grid=(n // D_MODEL,),
        in_specs=[
            pl.BlockSpec((B, D_MODEL), lambda j: (0, 0)),
            pl.BlockSpec((D_MODEL, D_MODEL), lambda j: (0, j)),
            pl.BlockSpec((1, D_MODEL), lambda j: (0, j)),
        ],
        out_specs=pl.BlockSpec((B, D_MODEL), lambda j: (0, j)),
        out_shape=jax.ShapeDtypeStruct((B, n), jnp.float32),
        name="adaln",
    )(c, w, b[None, :])


def _norm_rows(x):
    mu = jnp.mean(x, axis=-1, keepdims=True)
    xc = x - mu
    var = jnp.mean(xc * xc, axis=-1, keepdims=True)
    return xc * lax.rsqrt(var + LN_EPS)


def _proj_kernel(x_ref, sc_ref, sh_ref, w_ref, o_ref, h_ref):
    @pl.when(pl.program_id(2) == 0)
    def _():
        h_ref[...] = (_norm_rows(x_ref[0]) * (1.0 + sc_ref[0]) + sh_ref[0]).astype(jnp.bfloat16)

    o_ref[0] = jnp.dot(h_ref[...], w_ref[...], preferred_element_type=jnp.float32).astype(o_ref.dtype)


def _input_proj(x, sc, sh, w):
    B, S, D = x.shape
    n = w.shape[1]
    return pl.pallas_call(
        _proj_kernel,
        grid=(B, S // PROJ_TM, n // PROJ_TN),
        in_specs=[
            pl.BlockSpec((1, PROJ_TM, D), lambda b, i, j: (b, i, 0)),
            pl.BlockSpec((1, 1, D), lambda b, i, j: (b, 0, 0)),
            pl.BlockSpec((1, 1, D), lambda b, i, j: (b, 0, 0)),
            pl.BlockSpec((D, PROJ_TN), lambda b, i, j: (0, j)),
        ],
        out_specs=pl.BlockSpec((1, PROJ_TM, PROJ_TN), lambda b, i, j: (b, i, j)),
        out_shape=jax.ShapeDtypeStruct((B, S, n), jnp.bfloat16),
        scratch_shapes=[pltpu.VMEM((PROJ_TM, D), jnp.bfloat16)],
        compiler_params=pltpu.CompilerParams(
            dimension_semantics=("arbitrary", "arbitrary", "arbitrary")),
        name="input_proj",
    )(x, sc[:, None, :], sh[:, None, :], w)


def _retention_consts():
    h = jnp.arange(RET_HEADS, dtype=jnp.float32)
    log_gamma = jnp.log(1.0 - 2.0 ** (-5.0 - h))
    pos = jnp.arange(CHUNK, dtype=jnp.float32)
    rel = jnp.abs(pos[:, None] - pos[None, :])
    intra = jnp.exp(log_gamma[:, None, None] * rel)
    q_decay = jnp.exp(log_gamma[:, None, None] * (pos[None, :, None] + 1.0))
    k_decay = jnp.exp(log_gamma[:, None, None] * (CHUNK - 1.0 - pos[None, :, None]))
    chunk_decay = jnp.broadcast_to(jnp.exp(log_gamma * CHUNK)[:, None, None], (RET_HEADS, 1, LANES))
    half = RET_HEAD_DIM // 2
    inv_freq = ROPE_THETA ** (-jnp.arange(half, dtype=jnp.float32) / half)
    return inv_freq[None, :], intra, q_decay, k_decay, chunk_decay


def _retention_kernel(q_ref, k_ref, v_ref, g_ref, pos_ref, invf_ref, intra_ref, qd_ref, kd_ref, cd_ref,
                      o_ref, state_ref):
    @pl.when(pl.program_id(1) == 0)
    def _():
        state_ref[...] = jnp.zeros(state_ref.shape, jnp.float32)

    half = RET_HEAD_DIM // 2
    chunk = pl.program_id(1)
    lane_start = pl.multiple_of((chunk * CHUNK // LANES) * LANES, LANES)
    pick = (lax.broadcasted_iota(jnp.int32, (CHUNK, LANES), 1)
            == lax.broadcasted_iota(jnp.int32, (CHUNK, LANES), 0) + (chunk * CHUNK) % LANES)

    def rotate(t, cos, sin):
        t1, t2 = t[:, :half], t[:, half:]
        return jnp.concatenate([t1 * cos - t2 * sin, t1 * sin + t2 * cos], axis=1)

    for b in range(RET_BATCH):
        pos_row = pos_ref[b, :, pl.ds(lane_start, LANES)]
        pos_col = jnp.sum(jnp.where(pick, pos_row, 0.0), axis=1, keepdims=True)
        ang = pos_col * invf_ref[...]
        cos, sin = jnp.cos(ang), jnp.sin(ang)
        for h in range(RET_HEADS):
            cols = slice(h * RET_HEAD_DIM, (h + 1) * RET_HEAD_DIM)
            q = rotate(q_ref[b, :, cols].astype(jnp.float32), cos, sin).astype(jnp.bfloat16)
            k = rotate(k_ref[b, :, cols].astype(jnp.float32), cos, sin) * (RET_HEAD_DIM ** -0.5)
            v = v_ref[b, :, cols]
            s = lax.dot_general(q, k.astype(jnp.bfloat16), (((1,), (1,)), ((), ())),
                                preferred_element_type=jnp.float32) * intra_ref[h]
            intra = jnp.dot(s.astype(jnp.bfloat16), v, preferred_element_type=jnp.float32)
            state = state_ref[b, h]
            cross = jnp.dot(q, state.astype(jnp.bfloat16), preferred_element_type=jnp.float32) * qd_ref[h]
            kd = (k * kd_ref[h]).astype(jnp.bfloat16)
            state_ref[b, h] = state * cd_ref[h][:, :1] + lax.dot_general(
                kd, v, (((0,), (0,)), ((), ())), preferred_element_type=jnp.float32)
            g = g_ref[b, :, cols].astype(jnp.float32)
            o_ref[b, :, cols] = _norm_rows(intra + cross) * (g * jax.nn.sigmoid(g))


def _retention(proj, pos, consts):
    B, S, _ = proj.shape
    blk = (RET_BATCH, CHUNK, RET_WIDTH)

    def col(j):
        return pl.BlockSpec(blk, lambda b, c: (b, c, j))

    def const(a):
        return pl.BlockSpec(a.shape, lambda b, c: (0,) * a.ndim)

    return pl.pallas_call(
        _retention_kernel,
        grid=(B // RET_BATCH, S // CHUNK),
        in_specs=[col(0), col(1), col(2), col(3),
                  pl.BlockSpec((RET_BATCH, 1, S), lambda b, c: (b, 0, 0))] + [const(a) for a in consts],
        out_specs=pl.BlockSpec(blk, lambda b, c: (b, c, 0)),
        out_shape=jax.ShapeDtypeStruct((B, S, RET_WIDTH), jnp.float32),
        scratch_shapes=[pltpu.VMEM((RET_BATCH, RET_HEADS, RET_HEAD_DIM, RET_HEAD_DIM), jnp.float32)],
        compiler_params=pltpu.CompilerParams(dimension_semantics=("arbitrary", "arbitrary")),
        name="retention",
    )(proj, proj, proj, proj, pos, *consts)


def _merge_kernel(cin_ref, halo_ref, ga_ref, gb_ref, ret_ref, x_ref, dw_ref, dwb_ref, cg_ref, cb_ref,
                  wc_ref, bc_ref, wo_ref, g1_ref, sc2_ref, sh2_ref, l1g_ref, l1b_ref,
                  x1_ref, h2_ref, y_buf, c_buf):
    def glu(a):
        a = a.astype(jnp.float32)
        return a[:, :CONV_WIDTH] * jax.nn.sigmoid(a[:, CONV_WIDTH:])

    first = pl.program_id(1) == 0
    halo = jnp.where(first, 0.0, glu(halo_ref[0]))
    main = glu(cin_ref[0])
    n_tiles = CONV_WIDTH // LANES
    for lt in range(n_tiles):
        lanes = slice(lt * LANES, (lt + 1) * LANES)
        y_buf[lt, 0:CONV_HALO, :] = halo[:, lanes]
        y_buf[lt, CONV_HALO:CONV_HALO + MERGE_TS, :] = main[:, lanes]

    shift = CONV_HALO - (CONV_TAPS - 1)

    def conv_rows(r, carry):
        row = r * (CONV_BLOCKS * SUBLANES)
        for lt in range(n_tiles):
            acc = [jnp.zeros((SUBLANES, LANES), jnp.float32) for _ in range(CONV_BLOCKS)]
            for j in range(CONV_TAPS):
                tap = jnp.broadcast_to(dw_ref[j, lt:lt + 1, :], (SUBLANES, LANES))
                for i in range(CONV_BLOCKS):
                    acc[i] = acc[i] + tap * y_buf[lt, pl.ds(row + i * SUBLANES + shift + j, SUBLANES), :]
            for i in range(CONV_BLOCKS):
                c_buf[lt, pl.ds(row + i * SUBLANES, SUBLANES), :] = acc[i]
        return carry

    lax.fori_loop(0, MERGE_TS // (CONV_BLOCKS * SUBLANES), conv_rows, 0)

    conv_dw_out = jnp.concatenate([c_buf[lt] for lt in range(n_tiles)], axis=1) + dwb_ref[...]
    c = _norm_rows(conv_dw_out) * cg_ref[...] + cb_ref[...]
    c = (c * jax.nn.sigmoid(c)).astype(jnp.bfloat16)
    conv = jnp.dot(c, wc_ref[...], preferred_element_type=jnp.float32) + bc_ref[...]
    merged = (jax.nn.sigmoid(ga_ref[0].astype(jnp.float32)) * ret_ref[0]
              + jax.nn.sigmoid(gb_ref[0].astype(jnp.float32)) * conv)
    mix = jnp.dot(merged.astype(jnp.bfloat16), wo_ref[...], preferred_element_type=jnp.float32)
    x1 = _norm_rows(DEEPNORM_ALPHA * x_ref[0] + g1_ref[0] * mix) * l1g_ref[...] + l1b_ref[...]
    x1_ref[0] = x1
    h2_ref[0] = _norm_rows(x1) * (1.0 + sc2_ref[0]) + sh2_ref[0]


def _merge(proj, ret, x, conv_dw, conv_dw_b, conv_ln_g, conv_ln_b, w_conv_out, b_conv_out, w_out,
           g1, sc2, sh2, ln1_g, ln1_b):
    B, S, D = x.shape
    glu_col = 4 * RET_WIDTH // (2 * CONV_WIDTH)
    gate_col = (4 * RET_WIDTH + 2 * CONV_WIDTH) // D
    halo_per_tile = MERGE_TS // CONV_HALO

    def tile(width, j):
        return pl.BlockSpec((1, MERGE_TS, width), lambda b, s: (b, s, j))

    def row(a):
        return pl.BlockSpec((1, a.shape[-1]), lambda b, s: (0, 0))

    def per_batch():
        return pl.BlockSpec((1, 1, D), lambda b, s: (b, 0, 0))

    def whole(a):
        return pl.BlockSpec(a.shape, lambda b, s: (0,) * a.ndim)

    rows = [conv_dw_b[None, :], conv_ln_g[None, :], conv_ln_b[None, :]]
    wc = w_conv_out.astype(jnp.bfloat16)
    dw3 = conv_dw.reshape(CONV_TAPS, CONV_WIDTH // LANES, LANES)
    wo = w_out.astype(jnp.bfloat16)
    out_spec = pl.BlockSpec((1, MERGE_TS, D), lambda b, s: (b, s, 0))
    return pl.pallas_call(
        _merge_kernel,
        grid=(B, S // MERGE_TS),
        in_specs=[
            tile(2 * CONV_WIDTH, glu_col),
            pl.BlockSpec((1, CONV_HALO, 2 * CONV_WIDTH),
                         lambda b, s: (b, jnp.maximum(s * halo_per_tile - 1, 0), glu_col)),
            tile(D, gate_col), tile(D, gate_col + 1), tile(D, 0), tile(D, 0),
            whole(dw3), row(rows[0]), row(rows[1]), row(rows[2]),
            whole(wc), row(b_conv_out[None, :]), whole(wo),
            per_batch(), per_batch(), per_batch(), row(ln1_g[None, :]), row(ln1_b[None, :]),
        ],
        out_specs=[out_spec, out_spec],
        out_shape=[jax.ShapeDtypeStruct((B, S, D), jnp.float32)] * 2,
        scratch_shapes=[
            pltpu.VMEM((CONV_WIDTH // LANES, CONV_HALO + MERGE_TS, LANES), jnp.float32),
            pltpu.VMEM((CONV_WIDTH // LANES, MERGE_TS, LANES), jnp.float32),
        ],
        compiler_params=pltpu.CompilerParams(dimension_semantics=("arbitrary", "arbitrary")),
        name="merge",
    )(proj, proj, proj, proj, ret, x, dw3, rows[0], rows[1], rows[2], wc, b_conv_out[None, :], wo,
      g1[:, None, :], sc2[:, None, :], sh2[:, None, :], ln1_g[None, :], ln1_b[None, :])


def _topk_rows(s, k):
    n_rows = s.shape[0]
    row = lax.broadcasted_iota(jnp.int32, s.shape, 0)
    vals, idxs = [], []
    for _ in range(k):
        m = jnp.max(s, axis=0, keepdims=True)
        i = jnp.min(jnp.where(s == m, row, n_rows), axis=0, keepdims=True)
        vals.append(m)
        idxs.append(i)
        s = jnp.where(row == i, -jnp.inf, s)
    return jnp.concatenate(vals, axis=0), jnp.concatenate(idxs, axis=0)


def _candidate_tiles():
    k = PEER_TOPK
    tiles = []
    for j in range(SUBLANES):
        n_i = k // (j + 1)
        for i0 in range(0, n_i, SUBLANES):
            tiles.append((i0, min(SUBLANES, n_i - i0), j, None))
    tiles.append((0, 1, None, SUBLANES))
    return tiles


def _router_kernel(h_ref, wqt_ref, keys_ref, idx_u_ref, idx_v_ref, gate_ref):
    k = PEER_TOPK
    hb = h_ref[...].astype(jnp.bfloat16)
    qt = lax.dot_general(wqt_ref[...], hb, (((1,), (1,)), ((), ())),
                         preferred_element_type=jnp.float32)
    sub = lax.broadcasted_iota(jnp.int32, (SUBLANES, TOK_BLOCK), 0)
    tiles = _candidate_tiles()
    tile_pos = []
    for i0, n_i, j, j0 in tiles:
        if j is not None:
            tile_pos.append(jnp.where(sub < n_i, (sub + i0) * k + j, k * k))
        else:
            tile_pos.append(sub + j0)
    all_e, all_gate = [], []
    for h in range(PEER_HEADS):
        tops = []
        for p in range(2):
            hp = 2 * h + p
            qhp = qt[hp * PEER_HALF:(hp + 1) * PEER_HALF, :].astype(jnp.bfloat16)
            s = jnp.dot(keys_ref[hp], qhp, preferred_element_type=jnp.float32)
            tops.append(_topk_rows(s, k))
        (s0, i0v), (s1, i1v) = tops
        e0 = i0v * N_KEYS
        cand_s, cand_e = [], []
        for (i0, n_i, j, j0), pos in zip(tiles, tile_pos):
            if j is not None:
                cs = s0[i0:i0 + SUBLANES, :] + s1[j:j + 1, :]
                ce = e0[i0:i0 + SUBLANES, :] + i1v[j:j + 1, :]
                cand_s.append(jnp.where(pos < k * k, cs, -jnp.inf))
            else:
                cs = s0[0:1, :] + s1[j0:j0 + SUBLANES, :]
                ce = e0[0:1, :] + i1v[j0:j0 + SUBLANES, :]
                cand_s.append(cs)
            cand_e.append(ce)
        cand_s = jnp.concatenate(cand_s, axis=0)
        cand_e = jnp.concatenate(cand_e, axis=0)
        cand_pos = jnp.concatenate(tile_pos, axis=0)
        best_s, best_e = [], []
        for _ in range(k):
            m = jnp.max(cand_s, axis=0, keepdims=True)
            pos = jnp.min(jnp.where(cand_s == m, cand_pos, k * k), axis=0, keepdims=True)
            hit = cand_pos == pos
            best_s.append(m)
            best_e.append(jnp.max(jnp.where(hit, cand_e, -1), axis=0, keepdims=True))
            cand_s = jnp.where(hit, -jnp.inf, cand_s)
        best_s = jnp.concatenate(best_s, axis=0)
        ex = jnp.exp(best_s - best_s[0:1, :])
        all_gate.append(ex / jnp.sum(ex, axis=0, keepdims=True))
        all_e.append(jnp.concatenate(best_e, axis=0))
    pair = lax.broadcasted_iota(jnp.int32, (PEER_PAIRS, TOK_BLOCK), 0)
    rows = jnp.concatenate(all_e, axis=0) * ROWS_PER_EXPERT + TAB_FRONT
    idx_u_ref[...] = rows - jnp.where((pair & 4) == 0, 0, ROWS_PER_EXPERT)
    idx_v_ref[...] = rows - jnp.where((pair & 1) == 0, 0, ROWS_PER_EXPERT)
    gate_ref[...] = jnp.concatenate(all_gate, axis=0).T


def _peer_router(h, wqt, keys):
    T = h.shape[0]
    return pl.pallas_call(
        _router_kernel,
        grid=(T // TOK_BLOCK,),
        in_specs=[
            pl.BlockSpec((TOK_BLOCK, D_MODEL), lambda i: (i, 0)),
            pl.BlockSpec(wqt.shape, lambda i: (0, 0)),
            pl.BlockSpec(keys.shape, lambda i: (0, 0, 0)),
        ],
        out_specs=[
            pl.BlockSpec((PEER_PAIRS, TOK_BLOCK), lambda i: (i, 0)),
            pl.BlockSpec((PEER_PAIRS, TOK_BLOCK), lambda i: (i, 0)),
            pl.BlockSpec((TOK_BLOCK, PEER_PAIRS), lambda i: (i, 0)),
        ],
        out_shape=[
            jax.ShapeDtypeStruct((T // TOK_BLOCK * PEER_PAIRS, TOK_BLOCK), jnp.int32),
            jax.ShapeDtypeStruct((T // TOK_BLOCK * PEER_PAIRS, TOK_BLOCK), jnp.int32),
            jax.ShapeDtypeStruct((T, PEER_PAIRS), jnp.float32),
        ],
        compiler_params=pltpu.CompilerParams(dimension_semantics=("arbitrary",)),
        name="peer_router",
    )(h, wqt, keys)


def _pack_kernel(t_ref, o_ref):
    i = pl.program_id(0)
    pad = jnp.logical_or(i == 0, i == pl.num_programs(0) - 1)
    x = jnp.where(pad, 0.0, t_ref[...]).astype(jnp.bfloat16).astype(jnp.float32)
    bits = lax.bitcast_convert_type(x, jnp.int32)
    half = D_MODEL // 2
    word = ((bits[:, :half] >> 16) & 0xFFFF) | bits[:, half:]
    for s_ in range(ROWS_PER_EXPERT):
        o_ref[pl.ds(s_, PACK_ROWS, stride=ROWS_PER_EXPERT), :] = word[:, s_ * LANES:(s_ + 1) * LANES]


def _pack_table(tab):
    n = tab.shape[0]
    steps = n // PACK_ROWS
    return pl.pallas_call(
        _pack_kernel,
        grid=(steps + 2,),
        in_specs=[pl.BlockSpec((PACK_ROWS, D_MODEL), lambda i: (jnp.clip(i - 1, 0, steps - 1), 0))],
        out_specs=pl.BlockSpec((TAB_FRONT, LANES), lambda i: (i, 0)),
        out_shape=jax.ShapeDtypeStruct(((steps + 2) * TAB_FRONT, LANES), jnp.int32),
        name="pack_table",
    )(tab)


def _unpack(c):
    lo = lax.bitcast_convert_type(c << 16, jnp.float32)
    hi = lax.bitcast_convert_type(c & HI_MASK, jnp.float32)
    return lo, hi


def _sublane_masks():
    sub = lax.broadcasted_iota(jnp.int32, (SUBLANES, LANES), 0)
    return sub < 4, (sub & 2) == 0, (sub & 1) == 0


def _idx_copy(idx_hbm, block, buf, sem):
    return pltpu.make_async_copy(idx_hbm.at[pl.ds(block * IDX_BLOCK, IDX_BLOCK)], buf, sem)


def _two_block_step(idx_hbm, idx_a, idx_b, sems, process):
    i = pl.program_id(0)

    @pl.when(i == 0)
    def _():
        _idx_copy(idx_hbm, 0, idx_a, sems.at[0]).start()

    _idx_copy(idx_hbm, 2 * i + 1, idx_b, sems.at[1]).start()
    _idx_copy(idx_hbm, 2 * i, idx_a, sems.at[0]).wait()
    process(idx_a, 0)

    @pl.when(i + 1 < pl.num_programs(0))
    def _():
        _idx_copy(idx_hbm, 2 * i + 2, idx_a, sems.at[0]).start()

    _idx_copy(idx_hbm, 2 * i + 1, idx_b, sems.at[1]).wait()
    process(idx_b, 1)


def _pair_rows(idx_buf, tok, p):
    return idx_buf.at[pl.ds(p * TOK_BLOCK, TOK_BLOCK)][tok]


def _peer_dots_kernel(idx_hbm, x_ref, gate_ref, tab_ref, w_ref, idx_a, idx_b, sems, q_buf, d_buf):
    m4, m2, m1 = _sublane_masks()
    lane = lax.broadcasted_iota(jnp.int32, (SUBLANES, TOK_BLOCK), 1)
    groups = [slice(g * SUBLANES, (g + 1) * SUBLANES) for g in range(PEER_PAIRS // SUBLANES)]

    def process(idx_buf, half):
        base = half * TOK_BLOCK
        q_buf[1] = jnp.zeros((PEER_PAIRS, LANES), jnp.float32)
        d_buf[...] = jnp.zeros((PEER_PAIRS, TOK_BLOCK), jnp.float32)

        def step(t, carry):
            for rows in groups:
                col = jnp.sum(q_buf[(t + 1) & 1, rows, :], axis=1, keepdims=True)
                d_buf[rows, :] = jnp.where(lane == t - 1, col, d_buf[rows, :])

            tok = jnp.minimum(t, TOK_BLOCK - 1)
            xt = x_ref[base + tok]
            xr = pltpu.roll(xt, 4, axis=0)
            xlo = jnp.where(m4, xt, xr)
            xhi = jnp.where(m4, xr, xt)

            def prod(pa):
                wa = tab_ref[pl.ds(_pair_rows(idx_buf, tok, pa), SUBLANES), :]
                wb = tab_ref[pl.ds(_pair_rows(idx_buf, tok, pa + 4), SUBLANES), :]
                lo, hi = _unpack(jnp.where(m4, wa, wb))
                return lo * xlo + hi * xhi

            for rows in groups:
                p = rows.start
                v = [prod(p + a) for a in (0, 2, 1, 3)]
                w = [jnp.where(m2, v1, pltpu.roll(v2, 2, axis=0)) + jnp.where(m2, pltpu.roll(v1, 6, axis=0), v2)
                     for v1, v2 in ((v[0], v[1]), (v[2], v[3]))]
                q_buf[t & 1, rows, :] = (jnp.where(m1, w[0], pltpu.roll(w[1], 1, axis=0))
                                         + jnp.where(m1, pltpu.roll(w[0], 7, axis=0), w[1]))
            return carry

        lax.fori_loop(0, TOK_BLOCK + 1, step, 0)

        dots = d_buf[...].T
        act = 0.5 * dots * (1.0 + lax.erf(dots * (2.0 ** -0.5)))
        out = pl.ds(base, TOK_BLOCK)
        w_ref[out, :] = act * gate_ref[out, :]

    _two_block_step(idx_hbm, idx_a, idx_b, sems, process)


def _peer_scratch():
    return [
        pltpu.SMEM((IDX_BLOCK,), jnp.int32),
        pltpu.SMEM((IDX_BLOCK,), jnp.int32),
        pltpu.SemaphoreType.DMA((2,)),
    ]


def _peer_dots(idx, x, gate, tab):
    T = x.shape[0]
    return pl.pallas_call(
        _peer_dots_kernel,
        grid=(T // (2 * TOK_BLOCK),),
        in_specs=[
            pl.BlockSpec(memory_space=pl.ANY),
            pl.BlockSpec((2 * TOK_BLOCK, SUBLANES, LANES), lambda i: (i, 0, 0)),
            pl.BlockSpec((2 * TOK_BLOCK, PEER_PAIRS), lambda i: (i, 0)),
            pl.BlockSpec(memory_space=pltpu.VMEM),
        ],
        out_specs=pl.BlockSpec((2 * TOK_BLOCK, PEER_PAIRS), lambda i: (i, 0)),
        out_shape=jax.ShapeDtypeStruct((T, PEER_PAIRS), jnp.float32),
        scratch_shapes=_peer_scratch() + [
            pltpu.VMEM((2, PEER_PAIRS, LANES), jnp.float32),
            pltpu.VMEM((PEER_PAIRS, TOK_BLOCK), jnp.float32),
        ],
        compiler_params=pltpu.CompilerParams(
            dimension_semantics=("arbitrary",), vmem_limit_bytes=PEER_VMEM_LIMIT),
        name="peer_dots",
    )(idx.reshape(-1), x.reshape(T, SUBLANES, LANES), gate, tab)


def _peer_mix_kernel(idx_hbm, w_ref, tab_ref, out_ref, idx_a, idx_b, sems, m_buf):
    m4, _, _ = _sublane_masks()
    eye = (lax.broadcasted_iota(jnp.int32, (PEER_PAIRS, PEER_PAIRS), 0)
           == lax.broadcasted_iota(jnp.int32, (PEER_PAIRS, PEER_PAIRS), 1))
    ones = jnp.ones((PEER_PAIRS, LANES), jnp.bfloat16)
    n_slots = 2 * MIX_UNROLL

    def process(idx_buf, half):
        base = half * TOK_BLOCK

        def spread_weights(tok, slot):
            a = jnp.where(eye, w_ref[pl.ds(base + tok, 1), :], 0.0)
            a_hi = a.astype(jnp.bfloat16)
            a_lo = (a - a_hi.astype(jnp.float32)).astype(jnp.bfloat16)
            m = (jnp.dot(a_hi, ones, preferred_element_type=jnp.float32)
                 + jnp.dot(a_lo, ones, preferred_element_type=jnp.float32))
            for g in range(PEER_PAIRS // SUBLANES):
                rows = m[g * SUBLANES:(g + 1) * SUBLANES, :]
                for r in range(ROWS_PER_EXPERT):
                    m_buf[slot, pl.ds(g * SUBLANES * ROWS_PER_EXPERT + r, SUBLANES, stride=ROWS_PER_EXPERT), :] = rows

        for tok in range(n_slots):
            spread_weights(tok, tok)

        def token(t, slot):
            acc_lo = jnp.zeros((SUBLANES, LANES), jnp.float32)
            acc_hi = jnp.zeros((SUBLANES, LANES), jnp.float32)
            for j in range(PEER_PAIRS // 2):
                wa = tab_ref[pl.ds(_pair_rows(idx_buf, t, 2 * j), SUBLANES), :]
                wb = tab_ref[pl.ds(_pair_rows(idx_buf, t, 2 * j + 1), SUBLANES), :]
                lo, hi = _unpack(jnp.where(m4, wa, wb))
                wt = m_buf[slot, j * SUBLANES:(j + 1) * SUBLANES, :]
                acc_lo = acc_lo + wt * lo
                acc_hi = acc_hi + wt * hi
            lo = acc_lo + pltpu.roll(acc_lo, 4, axis=0)
            hi = acc_hi + pltpu.roll(acc_hi, 4, axis=0)
            out_ref[base + t] = jnp.where(m4, lo, hi)

        def tokens(k, carry):
            t0 = k * n_slots
            for grp in range(2):
                slots = range(grp * MIX_UNROLL, (grp + 1) * MIX_UNROLL)
                for s_ in slots:
                    token(t0 + s_, s_)
                for s_ in slots:
                    spread_weights(jnp.minimum(t0 + n_slots + s_, TOK_BLOCK - 1), s_)
            return carry

        lax.fori_loop(0, TOK_BLOCK // n_slots, tokens, 0)

    _two_block_step(idx_hbm, idx_a, idx_b, sems, process)


def _peer_mix(idx, w, tab):
    T = w.shape[0]
    out = pl.pallas_call(
        _peer_mix_kernel,
        grid=(T // (2 * TOK_BLOCK),),
        in_specs=[
            pl.BlockSpec(memory_space=pl.ANY),
            pl.BlockSpec((2 * TOK_BLOCK, PEER_PAIRS), lambda i: (i, 0)),
            pl.BlockSpec(memory_space=pltpu.VMEM),
        ],
        out_specs=pl.BlockSpec((2 * TOK_BLOCK, SUBLANES, LANES), lambda i: (i, 0, 0)),
        out_shape=jax.ShapeDtypeStruct((T, SUBLANES, LANES), jnp.float32),
        scratch_shapes=_peer_scratch() + [pltpu.VMEM((2 * MIX_UNROLL, ROWS_PER_EXPERT * PEER_PAIRS, LANES), jnp.float32)],
        compiler_params=pltpu.CompilerParams(
            dimension_semantics=("arbitrary",), vmem_limit_bytes=PEER_VMEM_LIMIT),
        name="peer_mix",
    )(idx.reshape(-1), w, tab)
    return out.reshape(T, D_MODEL)


def _peer(h, wq, subkeys, u_tab, v_tab):
    B, S, D = h.shape
    hf = h.reshape(B * S, D)
    wqt = wq.T.astype(jnp.bfloat16)
    keys = subkeys.reshape(PEER_HEADS * 2, N_KEYS, PEER_HALF).astype(jnp.bfloat16)
    idx_u, idx_v, gate = _peer_router(hf, wqt, keys)
    w = _peer_dots(idx_u, hf, gate, _pack_table(u_tab))
    return _peer_mix(idx_v, w, _pack_table(v_tab)).reshape(B, S, D)


def _final_norm_kernel(x_ref, f_ref, g2_ref, gain_ref, bias_ref, o_ref):
    y = DEEPNORM_ALPHA * x_ref[0] + g2_ref[0] * f_ref[0]
    mu = jnp.mean(y, axis=-1, keepdims=True)
    var = jnp.mean(jnp.square(y - mu), axis=-1, keepdims=True)
    o_ref[0] = (y - mu) * lax.rsqrt(var + LN_EPS) * gain_ref[...] + bias_ref[...]


def _final_norm(x, ffn, g2, gain, bias):
    B, S, D = x.shape
    ts = 512
    return pl.pallas_call(
        _final_norm_kernel,
        grid=(B, S // ts),
        in_specs=[
            pl.BlockSpec((1, ts, D), lambda b, s: (b, s, 0)),
            pl.BlockSpec((1, ts, D), lambda b, s: (b, s, 0)),
            pl.BlockSpec((1, 1, D), lambda b, s: (b, 0, 0)),
            pl.BlockSpec((1, D), lambda b, s: (0, 0)),
            pl.BlockSpec((1, D), lambda b, s: (0, 0)),
        ],
        out_specs=pl.BlockSpec((1, ts, D), lambda b, s: (b, s, 0)),
        out_shape=jax.ShapeDtypeStruct((B, S, D), jnp.float32),
        name="final_norm",
    )(x, ffn, g2[:, None, :], gain[None, :], bias[None, :])


def kernel(x, c, positions, w_ada, b_ada, w_in, conv_dw, conv_dw_b, conv_ln_g, conv_ln_b, w_conv_out, b_conv_out, w_out, ln1_g, ln1_b, peer_wq, peer_subkeys, peer_u, peer_v, ln2_g, ln2_b):
    l = 0
    mod = _adaln(c, w_ada[l], b_ada[l])
    sh1, sc1, g1, sh2, sc2, g2 = jnp.split(mod, 6, axis=-1)
    proj = _input_proj(x, sc1, sh1, w_in[l].astype(jnp.bfloat16))
    ret = _retention(proj, positions.astype(jnp.float32)[:, None, :], _retention_consts())
    x1, h2 = _merge(proj, ret, x, conv_dw[l], conv_dw_b[l], conv_ln_g[l], conv_ln_b[l],
                    w_conv_out[l], b_conv_out[l], w_out[l], g1, sc2, sh2, ln1_g[l], ln1_b[l])
    ffn = _peer(h2, peer_wq[l], peer_subkeys[l], peer_u[l], peer_v[l])
    return _final_norm(x1, ffn, g2, ln2_g[l], ln2_b[l])
```

```python
import jax
import jax.numpy as jnp
from jax import lax
from jax.experimental import pallas as pl
from jax.experimental.pallas import tpu as pltpu

D_MODEL = 1024
CHUNK = 64
RET_HEADS = 4
RET_HEAD_DIM = 256
RET_WIDTH = RET_HEADS * RET_HEAD_DIM
ROPE_THETA = 10000.0
CONV_WIDTH = D_MODEL
CONV_TAPS = 31
PEER_HEADS = 8
PEER_HALF = 128
N_KEYS = 128
N_EXPERTS = N_KEYS * N_KEYS
PEER_TOPK = 16
PEER_PAIRS = PEER_HEADS * PEER_TOPK
DEPTH = 1
DEEPNORM_ALPHA = (2.0 * DEPTH) ** 0.25
LN_EPS = 1e-5

SUBLANES = 8
LANES = 128
ROWS_PER_EXPERT = D_MODEL // (2 * LANES)
PACK_ROWS = 256
TAB_FRONT = PACK_ROWS * ROWS_PER_EXPERT
HI_MASK = -65536
TOK_BLOCK = 128
IDX_BLOCK = PEER_PAIRS * TOK_BLOCK
MIX_UNROLL = 1
PEER_VMEM_LIMIT = 52 * 1024 * 1024
PROJ_TM = 1024
PROJ_TN = 4096
RET_BATCH = 4
MERGE_TS = 512
CONV_HALO = 32
CONV_BLOCKS = 4


def _adaln_kernel(c_ref, w_ref, b_ref, o_ref):
    c = c_ref[...]
    cond = (c * jax.nn.sigmoid(c)).astype(jnp.bfloat16)
    o_ref[...] = jnp.dot(cond, w_ref[...].astype(jnp.bfloat16),
                         preferred_element_type=jnp.float32) + b_ref[...]


def _adaln(c, w, b):
    B = c.shape[0]
    n = w.shape[1]
    return pl.pallas_call(
        _adaln_kernel,
        grid=(n // D_MODEL,),
        in_specs=[
            pl.BlockSpec((B, D_MODEL), lambda j: (0, 0)),
            pl.BlockSpec((D_MODEL, D_MODEL), lambda j: (0, j)),
            pl.BlockSpec((1, D_MODEL), lambda j: (0, j)),
        ],
        out_specs=pl.BlockSpec((B, D_MODEL), lambda j: (0, j)),
        out_shape=jax.ShapeDtypeStruct((B, n), jnp.float32),
        name="adaln",
    )(c, w, b[None, :])


def _norm_rows(x):
    mu = jnp.mean(x, axis=-1, keepdims=True)
    xc = x - mu
    var = jnp.mean(xc * xc, axis=-1, keepdims=True)
    return xc * lax.rsqrt(var + LN_EPS)


def _proj_kernel(x_ref, sc_ref, sh_ref, w_ref, o_ref, h_ref):
    @pl.when(pl.program_id(2) == 0)
    def _():
        h_ref[...] = (_norm_rows(x_ref[0]) * (1.0 + sc_ref[0]) + sh_ref[0]).astype(jnp.bfloat16)

    o_ref[0] = jnp.dot(h_ref[...], w_ref[...], preferred_element_type=jnp.float32).astype(o_ref.dtype)


def _input_proj(x, sc, sh, w):
    B, S, D = x.shape
    n = w.shape[1]
    return pl.pallas_call(
        _proj_kernel,
        grid=(B, S // PROJ_TM, n // PROJ_TN),
        in_specs=[
            pl.BlockSpec((1, PROJ_TM, D), lambda b, i, j: (b, i, 0)),
            pl.BlockSpec((1, 1, D), lambda b, i, j: (b, 0, 0)),
            pl.BlockSpec((1, 1, D), lambda b, i, j: (b, 0, 0)),
            pl.BlockSpec((D, PROJ_TN), lambda b, i, j: (0, j)),
        ],
        out_specs=pl.BlockSpec((1, PROJ_TM, PROJ_TN), lambda b, i, j: (b, i, j)),
        out_shape=jax.ShapeDtypeStruct((B, S, n), jnp.bfloat16),
        scratch_shapes=[pltpu.VMEM((PROJ_TM, D), jnp.bfloat16)],
        compiler_params=pltpu.CompilerParams(
            dimension_semantics=("arbitrary", "arbitrary", "arbitrary")),
        name="input_proj",
    )(x, sc[:, None, :], sh[:, None, :], w)


def _retention_consts():
    h = jnp.arange(RET_HEADS, dtype=jnp.float32)
    log_gamma = jnp.log(1.0 - 2.0 ** (-5.0 - h))
    pos = jnp.arange(CHUNK, dtype=jnp.float32)
    rel = jnp.abs(pos[:, None] - pos[None, :])
    intra = jnp.exp(log_gamma[:, None, None] * rel)
    q_decay = jnp.exp(log_gamma[:, None, None] * (pos[None, :, None] + 1.0))
    k_decay = jnp.exp(log_gamma[:, None, None] * (CHUNK - 1.0 - pos[None, :, None]))
    chunk_decay = jnp.broadcast_to(jnp.exp(log_gamma * CHUNK)[:, None, None], (RET_HEADS, 1, LANES))
    half = RET_HEAD_DIM // 2
    inv_freq = ROPE_THETA ** (-jnp.arange(half, dtype=jnp.float32) / half)
    return inv_freq[None, :], intra, q_decay, k_decay, chunk_decay


def _retention_kernel(q_ref, k_ref, v_ref, g_ref, pos_ref, invf_ref, intra_ref, qd_ref, kd_ref, cd_ref,
                      o_ref, state_ref):
    @pl.when(pl.program_id(1) == 0)
    def _():
        state_ref[...] = jnp.zeros(state_ref.shape, jnp.float32)

    half = RET_HEAD_DIM // 2
    chunk = pl.program_id(1)
    lane_start = pl.multiple_of((chunk * CHUNK // LANES) * LANES, LANES)
    pick = (lax.broadcasted_iota(jnp.int32, (CHUNK, LANES), 1)
            == lax.broadcasted_iota(jnp.int32, (CHUNK, LANES), 0) + (chunk * CHUNK) % LANES)

    def rotate(t, cos, sin):
        t1, t2 = t[:, :half], t[:, half:]
        return jnp.concatenate([t1 * cos - t2 * sin, t1 * sin + t2 * cos], axis=1)

    for b in range(RET_BATCH):
        pos_row = pos_ref[b, :, pl.ds(lane_start, LANES)]
        pos_col = jnp.sum(jnp.where(pick, pos_row, 0.0), axis=1, keepdims=True)
        ang = pos_col * invf_ref[...]
        cos, sin = jnp.cos(ang), jnp.sin(ang)
        for h in range(RET_HEADS):
            cols = slice(h * RET_HEAD_DIM, (h + 1) * RET_HEAD_DIM)
            q = rotate(q_ref[b, :, cols].astype(jnp.float32), cos, sin).astype(jnp.bfloat16)
            k = rotate(k_ref[b, :, cols].astype(jnp.float32), cos, sin) * (RET_HEAD_DIM ** -0.5)
            v = v_ref[b, :, cols]
            s = lax.dot_general(q, k.astype(jnp.bfloat16), (((1,), (1,)), ((), ())),
                                preferred_element_type=jnp.float32) * intra_ref[h]
            intra = jnp.dot(s.astype(jnp.bfloat16), v, preferred_element_type=jnp.float32)
            state = state_ref[b, h]
            cross = jnp.dot(q, state.astype(jnp.bfloat16), preferred_element_type=jnp.float32) * qd_ref[h]
            kd = (k * kd_ref[h]).astype(jnp.bfloat16)
            state_ref[b, h] = state * cd_ref[h][:, :1] + lax.dot_general(
                kd, v, (((0,), (0,)), ((), ())), preferred_element_type=jnp.float32)
            g = g_ref[b, :, cols].astype(jnp.float32)
            o_ref[b, :, cols] = _norm_rows(intra + cross) * (g * jax.nn.sigmoid(g))


def _retention(proj, pos, consts):
    B, S, _ = proj.shape
    blk = (RET_BATCH, CHUNK, RET_WIDTH)

    def col(j):
        return pl.BlockSpec(blk, lambda b, c: (b, c, j))

    def const(a):
        return pl.BlockSpec(a.shape, lambda b, c: (0,) * a.ndim)

    return pl.pallas_call(
        _retention_kernel,
        grid=(B // RET_BATCH, S // CHUNK),
        in_specs=[col(0), col(1), col(2), col(3),
                  pl.BlockSpec((RET_BATCH, 1, S), lambda b, c: (b, 0, 0))] + [const(a) for a in consts],
        out_specs=pl.BlockSpec(blk, lambda b, c: (b, c, 0)),
        out_shape=jax.ShapeDtypeStruct((B, S, RET_WIDTH), jnp.float32),
        scratch_shapes=[pltpu.VMEM((RET_BATCH, RET_HEADS, RET_HEAD_DIM, RET_HEAD_DIM), jnp.float32)],
        compiler_params=pltpu.CompilerParams(dimension_semantics=("arbitrary", "arbitrary")),
        name="retention",
    )(proj, proj, proj, proj, pos, *consts)


def _merge_kernel(cin_ref, halo_ref, ga_ref, gb_ref, ret_ref, x_ref, dw_ref, dwb_ref, cg_ref, cb_ref,
                  wc_ref, bc_ref, wo_ref, g1_ref, sc2_ref, sh2_ref, l1g_ref, l1b_ref,
                  x1_ref, h2_ref, y_buf, c_buf):
    def glu(a):
        a = a.astype(jnp.float32)
        return a[:, :CONV_WIDTH] * jax.nn.sigmoid(a[:, CONV_WIDTH:])

    first = pl.program_id(1) == 0
    halo = jnp.where(first, 0.0, glu(halo_ref[0]))
    main = glu(cin_ref[0])
    n_tiles = CONV_WIDTH // LANES
    for lt in range(n_tiles):
        lanes = slice(lt * LANES, (lt + 1) * LANES)
        y_buf[lt, 0:CONV_HALO, :] = halo[:, lanes]
        y_buf[lt, CONV_HALO:CONV_HALO + MERGE_TS, :] = main[:, lanes]

    shift = CONV_HALO - (CONV_TAPS - 1)

    def conv_rows(r, carry):
        row = r * (CONV_BLOCKS * SUBLANES)
        for lt in range(n_tiles):
            acc = [jnp.zeros((SUBLANES, LANES), jnp.float32) for _ in range(CONV_BLOCKS)]
            for j in range(CONV_TAPS):
                tap = jnp.broadcast_to(dw_ref[j, lt:lt + 1, :], (SUBLANES, LANES))
                for i in range(CONV_BLOCKS):
                    acc[i] = acc[i] + tap * y_buf[lt, pl.ds(row + i * SUBLANES + shift + j, SUBLANES), :]
            for i in range(CONV_BLOCKS):
                c_buf[lt, pl.ds(row + i * SUBLANES, SUBLANES), :] = acc[i]
        return carry

    lax.fori_loop(0, MERGE_TS // (CONV_BLOCKS * SUBLANES), conv_rows, 0)

    conv_dw_out = jnp.concatenate([c_buf[lt] for lt in range(n_tiles)], axis=1) + dwb_ref[...]
    c = _norm_rows(conv_dw_out) * cg_ref[...] + cb_ref[...]
    c = (c * jax.nn.sigmoid(c)).astype(jnp.bfloat16)
    conv = jnp.dot(c, wc_ref[...], preferred_element_type=jnp.float32) + bc_ref[...]
    merged = (jax.nn.sigmoid(ga_ref[0].astype(jnp.float32)) * ret_ref[0]
              + jax.nn.sigmoid(gb_ref[0].astype(jnp.float32)) * conv)
    mix = jnp.dot(merged.astype(jnp.bfloat16), wo_ref[...], preferred_element_type=jnp.float32)
    x1 = _norm_rows(DEEPNORM_ALPHA * x_ref[0] + g1_ref[0] * mix) * l1g_ref[...] + l1b_ref[...]
    x1_ref[0] = x1
    h2_ref[0] = _norm_rows(x1) * (1.0 + sc2_ref[0]) + sh2_ref[0]


def _merge(proj, ret, x, conv_dw, conv_dw_b, conv_ln_g, conv_ln_b, w_conv_out, b_conv_out, w_out,
           g1, sc2, sh2, ln1_g, ln1_b):
    B, S, D = x.shape
    glu_col = 4 * RET_WIDTH // (2 * CONV_WIDTH)
    gate_col = (4 * RET_WIDTH + 2 * CONV_WIDTH) // D
    halo_per_tile = MERGE_TS // CONV_HALO

    def tile(width, j):
        return pl.BlockSpec((1, MERGE_TS, width), lambda b, s: (b, s, j))

    def row(a):
        return pl.BlockSpec((1, a.shape[-1]), lambda b, s: (0, 0))

    def per_batch():
        return pl.BlockSpec((1, 1, D), lambda b, s: (b, 0, 0))

    def whole(a):
        return pl.BlockSpec(a.shape, lambda b, s: (0,) * a.ndim)

    rows = [conv_dw_b[None, :], conv_ln_g[None, :], conv_ln_b[None, :]]
    wc = w_conv_out.astype(jnp.bfloat16)
    dw3 = conv_dw.reshape(CONV_TAPS, CONV_WIDTH // LANES, LANES)
    wo = w_out.astype(jnp.bfloat16)
    out_spec = pl.BlockSpec((1, MERGE_TS, D), lambda b, s: (b, s, 0))
    return pl.pallas_call(
        _merge_kernel,
        grid=(B, S // MERGE_TS),
        in_specs=[
            tile(2 * CONV_WIDTH, glu_col),
            pl.BlockSpec((1, CONV_HALO, 2 * CONV_WIDTH),
                         lambda b, s: (b, jnp.maximum(s * halo_per_tile - 1, 0), glu_col)),
            tile(D, gate_col), tile(D, gate_col + 1), tile(D, 0), tile(D, 0),
            whole(dw3), row(rows[0]), row(rows[1]), row(rows[2]),
            whole(wc), row(b_conv_out[None, :]), whole(wo),
            per_batch(), per_batch(), per_batch(), row(ln1_g[None, :]), row(ln1_b[None, :]),
        ],
        out_specs=[out_spec, out_spec],
        out_shape=[jax.ShapeDtypeStruct((B, S, D), jnp.float32)] * 2,
        scratch_shapes=[
            pltpu.VMEM((CONV_WIDTH // LANES, CONV_HALO + MERGE_TS, LANES), jnp.float32),
            pltpu.VMEM((CONV_WIDTH // LANES, MERGE_TS, LANES), jnp.float32),
        ],
        compiler_params=pltpu.CompilerParams(dimension_semantics=("arbitrary", "arbitrary")),
        name="merge",
    )(proj, proj, proj, proj, ret, x, dw3, rows[0], rows[1], rows[2], wc, b_conv_out[None, :], wo,
      g1[:, None, :], sc2[:, None, :], sh2[:, None, :], ln1_g[None, :], ln1_b[None, :])


def _topk_rows(s, k):
    n_rows = s.shape[0]
    row = lax.broadcasted_iota(jnp.int32, s.shape, 0)
    vals, idxs = [], []
    for _ in range(k):
        m = jnp.max(s, axis=0, keepdims=True)
        i = jnp.min(jnp.where(s == m, row, n_rows), axis=0, keepdims=True)
        vals.append(m)
        idxs.append(i)
        s = jnp.where(row == i, -jnp.inf, s)
    return jnp.concatenate(vals, axis=0), jnp.concatenate(idxs, axis=0)


def _candidate_tiles():
    k = PEER_TOPK
    tiles = []
    for j in range(SUBLANES):
        n_i = k // (j + 1)
        for i0 in range(0, n_i, SUBLANES):
            tiles.append((i0, min(SUBLANES, n_i - i0), j, None))
    tiles.append((0, 1, None, SUBLANES))
    return tiles


def _router_kernel(h_ref, wqt_ref, keys_ref, idx_u_ref, idx_v_ref, gate_ref):
    k = PEER_TOPK
    hb = h_ref[...].astype(jnp.bfloat16)
    qt = lax.dot_general(wqt_ref[...], hb, (((1,), (1,)), ((), ())),
                         preferred_element_type=jnp.float32)
    sub = lax.broadcasted_iota(jnp.int32, (SUBLANES, TOK_BLOCK), 0)
    tiles = _candidate_tiles()
    tile_pos = []
    for i0, n_i, j, j0 in tiles:
        if j is not None:
            tile_pos.append(jnp.where(sub < n_i, (sub + i0) * k + j, k * k))
        else:
            tile_pos.append(sub + j0)
    all_e, all_gate = [], []
    for h in range(PEER_HEADS):
        tops = []
        for p in range(2):
            hp = 2 * h + p
            qhp = qt[hp * PEER_HALF:(hp + 1) * PEER_HALF, :].astype(jnp.bfloat16)
            s = jnp.dot(keys_ref[hp], qhp, preferred_element_type=jnp.float32)
            tops.append(_topk_rows(s, k))
        (s0, i0v), (s1, i1v) = tops
        e0 = i0v * N_KEYS
        cand_s, cand_e = [], []
        for (i0, n_i, j, j0), pos in zip(tiles, tile_pos):
            if j is not None:
                cs = s0[i0:i0 + SUBLANES, :] + s1[j:j + 1, :]
                ce = e0[i0:i0 + SUBLANES, :] + i1v[j:j + 1, :]
                cand_s.append(jnp.where(pos < k * k, cs, -jnp.inf))
            else:
                cs = s0[0:1, :] + s1[j0:j0 + SUBLANES, :]
                ce = e0[0:1, :] + i1v[j0:j0 + SUBLANES, :]
                cand_s.append(cs)
            cand_e.append(ce)
        cand_s = jnp.concatenate(cand_s, axis=0)
        cand_e = jnp.concatenate(cand_e, axis=0)
        cand_pos = jnp.concatenate(tile_pos, axis=0)
        best_s, best_e = [], []
        for _ in range(k):
            m = jnp.max(cand_s, axis=0, keepdims=True)
            pos = jnp.min(jnp.where(cand_s == m, cand_pos, k * k), axis=0, keepdims=True)
            hit = cand_pos == pos
            best_s.append(m)
            best_e.append(jnp.max(jnp.where(hit, cand_e, -1), axis=0, keepdims=True))
            cand_s = jnp.where(hit, -jnp.inf, cand_s)
        best_s = jnp.concatenate(best_s, axis=0)
        ex = jnp.exp(best_s - best_s[0:1, :])
        all_gate.append(ex / jnp.sum(ex, axis=0, keepdims=True))
        all_e.append(jnp.concatenate(best_e, axis=0))
    pair = lax.broadcasted_iota(jnp.int32, (PEER_PAIRS, TOK_BLOCK), 0)
    rows = jnp.concatenate(all_e, axis=0) * ROWS_PER_EXPERT + TAB_FRONT
    idx_u_ref[...] = rows - jnp.where((pair & 4) == 0, 0, ROWS_PER_EXPERT)
    idx_v_ref[...] = rows - jnp.where((pair & 1) == 0, 0, ROWS_PER_EXPERT)
    gate_ref[...] = jnp.concatenate(all_gate, axis=0).T


def _peer_router(h, wqt, keys):
    T = h.shape[0]
    return pl.pallas_call(
        _router_kernel,
        grid=(T // TOK_BLOCK,),
        in_specs=[
            pl.BlockSpec((TOK_BLOCK, D_MODEL), lambda i: (i, 0)),
            pl.BlockSpec(wqt.shape, lambda i: (0, 0)),
            pl.BlockSpec(keys.shape, lambda i: (0, 0, 0)),
        ],
        out_specs=[
            pl.BlockSpec((PEER_PAIRS, TOK_BLOCK), lambda i: (i, 0)),
            pl.BlockSpec((PEER_PAIRS, TOK_BLOCK), lambda i: (i, 0)),
            pl.BlockSpec((TOK_BLOCK, PEER_PAIRS), lambda i: (i, 0)),
        ],
        out_shape=[
            jax.ShapeDtypeStruct((T // TOK_BLOCK * PEER_PAIRS, TOK_BLOCK), jnp.int32),
            jax.ShapeDtypeStruct((T // TOK_BLOCK * PEER_PAIRS, TOK_BLOCK), jnp.int32),
            jax.ShapeDtypeStruct((T, PEER_PAIRS), jnp.float32),
        ],
        compiler_params=pltpu.CompilerParams(dimension_semantics=("arbitrary",)),
        name="peer_router",
    )(h, wqt, keys)


def _pack_kernel(t_ref, o_ref):
    i = pl.program_id(0)
    pad = jnp.logical_or(i == 0, i == pl.num_programs(0) - 1)
    x = jnp.where(pad, 0.0, t_ref[...]).astype(jnp.bfloat16).astype(jnp.float32)
    bits = lax.bitcast_convert_type(x, jnp.int32)
    half = D_MODEL // 2
    word = ((bits[:, :half] >> 16) & 0xFFFF) | bits[:, half:]
    for s_ in range(ROWS_PER_EXPERT):
        o_ref[pl.ds(s_, PACK_ROWS, stride=ROWS_PER_EXPERT), :] = word[:, s_ * LANES:(s_ + 1) * LANES]


def _pack_table(tab):
    n = tab.shape[0]
    steps = n // PACK_ROWS
    return pl.pallas_call(
        _pack_kernel,
        grid=(steps + 2,),
        in_specs=[pl.BlockSpec((PACK_ROWS, D_MODEL), lambda i: (jnp.clip(i - 1, 0, steps - 1), 0))],
        out_specs=pl.BlockSpec((TAB_FRONT, LANES), lambda i: (i, 0)),
        out_shape=jax.ShapeDtypeStruct(((steps + 2) * TAB_FRONT, LANES), jnp.int32),
        name="pack_table",
    )(tab)


def _unpack(c):
    lo = lax.bitcast_convert_type(c << 16, jnp.float32)
    hi = lax.bitcast_convert_type(c & HI_MASK, jnp.float32)
    return lo, hi


def _sublane_masks():
    sub = lax.broadcasted_iota(jnp.int32, (SUBLANES, LANES), 0)
    return sub < 4, (sub & 2) == 0, (sub & 1) == 0


def _idx_copy(idx_hbm, block, buf, sem):
    return pltpu.make_async_copy(idx_hbm.at[pl.ds(block * IDX_BLOCK, IDX_BLOCK)], buf, sem)


def _two_block_step(idx_hbm, idx_a, idx_b, sems, process):
    i = pl.program_id(0)

    @pl.when(i == 0)
    def _():
        _idx_copy(idx_hbm, 0, idx_a, sems.at[0]).start()

    _idx_copy(idx_hbm, 2 * i + 1, idx_b, sems.at[1]).start()
    _idx_copy(idx_hbm, 2 * i, idx_a, sems.at[0]).wait()
    process(idx_a, 0)

    @pl.when(i + 1 < pl.num_programs(0))
    def _():
        _idx_copy(idx_hbm, 2 * i + 2, idx_a, sems.at[0]).start()

    _idx_copy(idx_hbm, 2 * i + 1, idx_b, sems.at[1]).wait()
    process(idx_b, 1)


def _pair_rows(idx_buf, tok, p):
    return idx_buf.at[pl.ds(p * TOK_BLOCK, TOK_BLOCK)][tok]


def _peer_dots_kernel(idx_hbm, x_ref, gate_ref, tab_ref, w_ref, idx_a, idx_b, sems, q_buf, d_buf):
    m4, m2, m1 = _sublane_masks()
    lane = lax.broadcasted_iota(jnp.int32, (SUBLANES, TOK_BLOCK), 1)
    groups = [slice(g * SUBLANES, (g + 1) * SUBLANES) for g in range(PEER_PAIRS // SUBLANES)]

    def process(idx_buf, half):
        base = half * TOK_BLOCK
        q_buf[1] = jnp.zeros((PEER_PAIRS, LANES), jnp.float32)
        d_buf[...] = jnp.zeros((PEER_PAIRS, TOK_BLOCK), jnp.float32)

        def step(t, carry):
            for rows in groups:
                col = jnp.sum(q_buf[(t + 1) & 1, rows, :], axis=1, keepdims=True)
                d_buf[rows, :] = jnp.where(lane == t - 1, col, d_buf[rows, :])

            tok = jnp.minimum(t, TOK_BLOCK - 1)
            xt = x_ref[base + tok]
            xr = pltpu.roll(xt, 4, axis=0)
            xlo = jnp.where(m4, xt, xr)
            xhi = jnp.where(m4, xr, xt)

            def prod(pa):
                wa = tab_ref[pl.ds(_pair_rows(idx_buf, tok, pa), SUBLANES), :]
                wb = tab_ref[pl.ds(_pair_rows(idx_buf, tok, pa + 4), SUBLANES), :]
                lo, hi = _unpack(jnp.where(m4, wa, wb))
                return lo * xlo + hi * xhi

            for rows in groups:
                p = rows.start
                v = [prod(p + a) for a in (0, 2, 1, 3)]
                w = [jnp.where(m2, v1, pltpu.roll(v2, 2, axis=0)) + jnp.where(m2, pltpu.roll(v1, 6, axis=0), v2)
                     for v1, v2 in ((v[0], v[1]), (v[2], v[3]))]
                q_buf[t & 1, rows, :] = (jnp.where(m1, w[0], pltpu.roll(w[1], 1, axis=0))
                                         + jnp.where(m1, pltpu.roll(w[0], 7, axis=0), w[1]))
            return carry

        lax.fori_loop(0, TOK_BLOCK + 1, step, 0)

        dots = d_buf[...].T
        act = 0.5 * dots * (1.0 + lax.erf(dots * (2.0 ** -0.5)))
        out = pl.ds(base, TOK_BLOCK)
        w_ref[out, :] = act * gate_ref[out, :]

    _two_block_step(idx_hbm, idx_a, idx_b, sems, process)


def _peer_scratch():
    return [
        pltpu.SMEM((IDX_BLOCK,), jnp.int32),
        pltpu.SMEM((IDX_BLOCK,), jnp.int32),
        pltpu.SemaphoreType.DMA((2,)),
    ]


def _peer_dots(idx, x, gate, tab):
    T = x.shape[0]
    return pl.pallas_call(
        _peer_dots_kernel,
        grid=(T // (2 * TOK_BLOCK),),
        in_specs=[
            pl.BlockSpec(memory_space=pl.ANY),
            pl.BlockSpec((2 * TOK_BLOCK, SUBLANES, LANES), lambda i: (i, 0, 0)),
            pl.BlockSpec((2 * TOK_BLOCK, PEER_PAIRS), lambda i: (i, 0)),
            pl.BlockSpec(memory_space=pltpu.VMEM),
        ],
        out_specs=pl.BlockSpec((2 * TOK_BLOCK, PEER_PAIRS), lambda i: (i, 0)),
        out_shape=jax.ShapeDtypeStruct((T, PEER_PAIRS), jnp.float32),
        scratch_shapes=_peer_scratch() + [
            pltpu.VMEM((2, PEER_PAIRS, LANES), jnp.float32),
            pltpu.VMEM((PEER_PAIRS, TOK_BLOCK), jnp.float32),
        ],
        compiler_params=pltpu.CompilerParams(
            dimension_semantics=("arbitrary",), vmem_limit_bytes=PEER_VMEM_LIMIT),
        name="peer_dots",
    )(idx.reshape(-1), x.reshape(T, SUBLANES, LANES), gate, tab)


def _peer_mix_kernel(idx_hbm, w_ref, tab_ref, out_ref, idx_a, idx_b, sems, m_buf):
    m4, _, _ = _sublane_masks()
    n_slots = 2 * MIX_UNROLL

    def process(idx_buf, half):
        base = half * TOK_BLOCK

        def spread_weights(tok, slot):
            m = jnp.broadcast_to(w_ref[pl.ds(base + tok, 1), :], (PEER_PAIRS, PEER_PAIRS)).T
            for g in range(PEER_PAIRS // SUBLANES):
                rows = m[g * SUBLANES:(g + 1) * SUBLANES, :]
                for r in range(ROWS_PER_EXPERT):
                    m_buf[slot, pl.ds(g * SUBLANES * ROWS_PER_EXPERT + r, SUBLANES, stride=ROWS_PER_EXPERT), :] = rows

        for tok in range(n_slots):
            spread_weights(tok, tok)

        def token(t, slot):
            acc_lo = jnp.zeros((SUBLANES, LANES), jnp.float32)
            acc_hi = jnp.zeros((SUBLANES, LANES), jnp.float32)
            for j in range(PEER_PAIRS // 2):
                wa = tab_ref[pl.ds(_pair_rows(idx_buf, t, 2 * j), SUBLANES), :]
                wb = tab_ref[pl.ds(_pair_rows(idx_buf, t, 2 * j + 1), SUBLANES), :]
                lo, hi = _unpack(jnp.where(m4, wa, wb))
                wt = m_buf[slot, j * SUBLANES:(j + 1) * SUBLANES, :]
                acc_lo = acc_lo + wt * lo
                acc_hi = acc_hi + wt * hi
            lo = acc_lo + pltpu.roll(acc_lo, 4, axis=0)
            hi = acc_hi + pltpu.roll(acc_hi, 4, axis=0)
            out_ref[base + t] = jnp.where(m4, lo, hi)

        def tokens(k, carry):
            t0 = k * n_slots
            for grp in range(2):
                slots = range(grp * MIX_UNROLL, (grp + 1) * MIX_UNROLL)
                for s_ in slots:
                    token(t0 + s_, s_)
                for s_ in slots:
                    spread_weights(jnp.minimum(t0 + n_slots + s_, TOK_BLOCK - 1), s_)
            return carry

        lax.fori_loop(0, TOK_BLOCK // n_slots, tokens, 0)

    _two_block_step(idx_hbm, idx_a, idx_b, sems, process)


def _peer_mix(idx, w, tab):
    T = w.shape[0]
    out = pl.pallas_call(
        _peer_mix_kernel,
        grid=(T // (2 * TOK_BLOCK),),
        in_specs=[
            pl.BlockSpec(memory_space=pl.ANY),
            pl.BlockSpec((2 * TOK_BLOCK, PEER_PAIRS), lambda i: (i, 0)),
            pl.BlockSpec(memory_space=pltpu.VMEM),
        ],
        out_specs=pl.BlockSpec((2 * TOK_BLOCK, SUBLANES, LANES), lambda i: (i, 0, 0)),
        out_shape=jax.ShapeDtypeStruct((T, SUBLANES, LANES), jnp.float32),
        scratch_shapes=_peer_scratch() + [pltpu.VMEM((2 * MIX_UNROLL, ROWS_PER_EXPERT * PEER_PAIRS, LANES), jnp.float32)],
        compiler_params=pltpu.CompilerParams(
            dimension_semantics=("arbitrary",), vmem_limit_bytes=PEER_VMEM_LIMIT),
        name="peer_mix",
    )(idx.reshape(-1), w, tab)
    return out.reshape(T, D_MODEL)


def _peer(h, wq, subkeys, u_tab, v_tab):
    B, S, D = h.shape
    hf = h.reshape(B * S, D)
    wqt = wq.T.astype(jnp.bfloat16)
    keys = subkeys.reshape(PEER_HEADS * 2, N_KEYS, PEER_HALF).astype(jnp.bfloat16)
    idx_u, idx_v, gate = _peer_router(hf, wqt, keys)
    w = _peer_dots(idx_u, hf, gate, _pack_table(u_tab))
    return _peer_mix(idx_v, w, _pack_table(v_tab)).reshape(B, S, D)


def _final_norm_kernel(x_ref, f_ref, g2_ref, gain_ref, bias_ref, o_ref):
    y = DEEPNORM_ALPHA * x_ref[0] + g2_ref[0] * f_ref[0]
    mu = jnp.mean(y, axis=-1, keepdims=True)
    var = jnp.mean(jnp.square(y - mu), axis=-1, keepdims=True)
    o_ref[0] = (y - mu) * lax.rsqrt(var + LN_EPS) * gain_ref[...] + bias_ref[...]


def _final_norm(x, ffn, g2, gain, bias):
    B, S, D = x.shape
    ts = 512
    return pl.pallas_call(
        _final_norm_kernel,
        grid=(B, S // ts),
        in_specs=[
            pl.BlockSpec((1, ts, D), lambda b, s: (b, s, 0)),
            pl.BlockSpec((1, ts, D), lambda b, s: (b, s, 0)),
            pl.BlockSpec((1, 1, D), lambda b, s: (b, 0, 0)),
            pl.BlockSpec((1, D), lambda b, s: (0, 0)),
            pl.BlockSpec((1, D), lambda b, s: (0, 0)),
        ],
        out_specs=pl.BlockSpec((1, ts, D), lambda b, s: (b, s, 0)),
        out_shape=jax.ShapeDtypeStruct((B, S, D), jnp.float32),
        name="final_norm",
    )(x, ffn, g2[:, None, :], gain[None, :], bias[None, :])


def kernel(x, c, positions, w_ada, b_ada, w_in, conv_dw, conv_dw_b, conv_ln_g, conv_ln_b, w_conv_out, b_conv_out, w_out, ln1_g, ln1_b, peer_wq, peer_subkeys, peer_u, peer_v, ln2_g, ln2_b):
    l = 0
    mod = _adaln(c, w_ada[l], b_ada[l])
    sh1, sc1, g1, sh2, sc2, g2 = jnp.split(mod, 6, axis=-1)
    proj = _input_proj(x, sc1, sh1, w_in[l].astype(jnp.bfloat16))
    ret = _retention(proj, positions.astype(jnp.float32)[:, None, :], _retention_consts())
    x1, h2 = _merge(proj, ret, x, conv_dw[l], conv_dw_b[l], conv_ln_g[l], conv_ln_b[l],
                    w_conv_out[l], b_conv_out[l], w_out[l], g1, sc2, sh2, ln1_g[l], ln1_b[l])
    ffn = _peer(h2, peer_wq[l], peer_subkeys[l], peer_u[l], peer_v[l])
    return _final_norm(x1, ffn, g2, ln2_g[l], ln2_b[l])
```

```python
import jax
import jax.numpy as jnp
from jax import lax
from jax.experimental import pallas as pl
from jax.experimental.pallas import tpu as pltpu

D_MODEL = 1024
CHUNK = 64
RET_HEADS = 4
RET_HEAD_DIM = 256
RET_WIDTH = RET_HEADS * RET_HEAD_DIM
ROPE_THETA = 10000.0
CONV_WIDTH = D_MODEL
CONV_TAPS = 31
PEER_HEADS = 8
PEER_HALF = 128
N_KEYS = 128
N_EXPERTS = N_KEYS * N_KEYS
PEER_TOPK = 16
PEER_PAIRS = PEER_HEADS * PEER_TOPK
DEPTH = 1
DEEPNORM_ALPHA = (2.0 * DEPTH) ** 0.25
LN_EPS = 1e-5

SUBLANES = 8
LANES = 128
ROWS_PER_EXPERT = D_MODEL // (2 * LANES)
PACK_ROWS = 256
TAB_FRONT = PACK_ROWS * ROWS_PER_EXPERT
HI_MASK = -65536
TOK_BLOCK = 128
IDX_BLOCK = PEER_PAIRS * TOK_BLOCK
MIX_UNROLL = 1
PEER_VMEM_LIMIT = 52 * 1024 * 1024
PROJ_TM = 1024
PROJ_TN = 4096
RET_BATCH = 4
MERGE_TS = 512
CONV_HALO = 32
CONV_BLOCKS = 4


def _adaln_kernel(c_ref, w_ref, b_ref, o_ref):
    c = c_ref[...]
    cond = (c * jax.nn.sigmoid(c)).astype(jnp.bfloat16)
    o_ref[...] = jnp.dot(cond, w_ref[...].astype(jnp.bfloat16),
                         preferred_element_type=jnp.float32) + b_ref[...]


def _adaln(c, w, b):
    B = c.shape[0]
    n = w.shape[1]
    return pl.pallas_call(
        _adaln_kernel,
        grid=(n // D_MODEL,),
        in_specs=[
            pl.BlockSpec((B, D_MODEL), lambda j: (0, 0)),
            pl.BlockSpec((D_MODEL, D_MODEL), lambda j: (0, j)),
            pl.BlockSpec((1, D_MODEL), lambda j: (0, j)),
        ],
        out_specs=pl.BlockSpec((B, D_MODEL), lambda j: (0, j)),
        out_shape=jax.ShapeDtypeStruct((B, n), jnp.float32),
        name="adaln",
    )(c, w, b[None, :])


def _norm_rows(x):
    mu = jnp.mean(x, axis=-1, keepdims=True)
    xc = x - mu
    var = jnp.mean(xc * xc, axis=-1, keepdims=True)
    return xc * lax.rsqrt(var + LN_EPS)


def _proj_kernel(x_ref, sc_ref, sh_ref, w_ref, o_ref, h_ref):
    @pl.when(pl.program_id(2) == 0)
    def _():
        h_ref[...] = (_norm_rows(x_ref[0]) * (1.0 + sc_ref[0]) + sh_ref[0]).astype(jnp.bfloat16)

    o_ref[0] = jnp.dot(h_ref[...], w_ref[...], preferred_element_type=jnp.float32).astype(o_ref.dtype)


def _input_proj(x, sc, sh, w):
    B, S, D = x.shape
    n = w.shape[1]
    return pl.pallas_call(
        _proj_kernel,
        grid=(B, S // PROJ_TM, n // PROJ_TN),
        in_specs=[
            pl.BlockSpec((1, PROJ_TM, D), lambda b, i, j: (b, i, 0)),
            pl.BlockSpec((1, 1, D), lambda b, i, j: (b, 0, 0)),
            pl.BlockSpec((1, 1, D), lambda b, i, j: (b, 0, 0)),
            pl.BlockSpec((D, PROJ_TN), lambda b, i, j: (0, j)),
        ],
        out_specs=pl.BlockSpec((1, PROJ_TM, PROJ_TN), lambda b, i, j: (b, i, j)),
        out_shape=jax.ShapeDtypeStruct((B, S, n), jnp.bfloat16),
        scratch_shapes=[pltpu.VMEM((PROJ_TM, D), jnp.bfloat16)],
        compiler_params=pltpu.CompilerParams(
            dimension_semantics=("arbitrary", "arbitrary", "arbitrary")),
        name="input_proj",
    )(x, sc[:, None, :], sh[:, None, :], w)


def _retention_consts():
    h = jnp.arange(RET_HEADS, dtype=jnp.float32)
    log_gamma = jnp.log(1.0 - 2.0 ** (-5.0 - h))
    pos = jnp.arange(CHUNK, dtype=jnp.float32)
    rel = jnp.abs(pos[:, None] - pos[None, :])
    intra = jnp.exp(log_gamma[:, None, None] * rel)
    q_decay = jnp.exp(log_gamma[:, None, None] * (pos[None, :, None] + 1.0))
    k_decay = jnp.exp(log_gamma[:, None, None] * (CHUNK - 1.0 - pos[None, :, None]))
    chunk_decay = jnp.broadcast_to(jnp.exp(log_gamma * CHUNK)[:, None, None], (RET_HEADS, 1, LANES))
    half = RET_HEAD_DIM // 2
    inv_freq = ROPE_THETA ** (-jnp.arange(half, dtype=jnp.float32) / half)
    return inv_freq[None, :], intra, q_decay, k_decay, chunk_decay


def _retention_kernel(q_ref, k_ref, v_ref, g_ref, pos_ref, invf_ref, intra_ref, qd_ref, kd_ref, cd_ref,
                      o_ref, state_ref):
    @pl.when(pl.program_id(1) == 0)
    def _():
        state_ref[...] = jnp.zeros(state_ref.shape, jnp.float32)

    half = RET_HEAD_DIM // 2
    chunk = pl.program_id(1)
    lane_start = pl.multiple_of((chunk * CHUNK // LANES) * LANES, LANES)
    pick = (lax.broadcasted_iota(jnp.int32, (CHUNK, LANES), 1)
            == lax.broadcasted_iota(jnp.int32, (CHUNK, LANES), 0) + (chunk * CHUNK) % LANES)

    def rotate(t, cos, sin):
        t1, t2 = t[:, :half], t[:, half:]
        return jnp.concatenate([t1 * cos - t2 * sin, t1 * sin + t2 * cos], axis=1)

    for b in range(RET_BATCH):
        pos_row = pos_ref[b, :, pl.ds(lane_start, LANES)]
        pos_col = jnp.sum(jnp.where(pick, pos_row, 0.0), axis=1, keepdims=True)
        ang = pos_col * invf_ref[...]
        cos, sin = jnp.cos(ang), jnp.sin(ang)
        for h in range(RET_HEADS):
            cols = slice(h * RET_HEAD_DIM, (h + 1) * RET_HEAD_DIM)
            q = rotate(q_ref[b, :, cols].astype(jnp.float32), cos, sin).astype(jnp.bfloat16)
            k = rotate(k_ref[b, :, cols].astype(jnp.float32), cos, sin) * (RET_HEAD_DIM ** -0.5)
            v = v_ref[b, :, cols]
            s = lax.dot_general(q, k.astype(jnp.bfloat16), (((1,), (1,)), ((), ())),
                                preferred_element_type=jnp.float32) * intra_ref[h]
            intra = jnp.dot(s.astype(jnp.bfloat16), v, preferred_element_type=jnp.float32)
            state = state_ref[b, h]
            cross = jnp.dot(q, state.astype(jnp.bfloat16), preferred_element_type=jnp.float32) * qd_ref[h]
            kd = (k * kd_ref[h]).astype(jnp.bfloat16)
            state_ref[b, h] = state * cd_ref[h][:, :1] + lax.dot_general(
                kd, v, (((0,), (0,)), ((), ())), preferred_element_type=jnp.float32)
            g = g_ref[b, :, cols].astype(jnp.float32)
            o_ref[b, :, cols] = _norm_rows(intra + cross) * (g * jax.nn.sigmoid(g))


def _retention(proj, pos, consts):
    B, S, _ = proj.shape
    blk = (RET_BATCH, CHUNK, RET_WIDTH)

    def col(j):
        return pl.BlockSpec(blk, lambda b, c: (b, c, j))

    def const(a):
        return pl.BlockSpec(a.shape, lambda b, c: (0,) * a.ndim)

    return pl.pallas_call(
        _retention_kernel,
        grid=(B // RET_BATCH, S // CHUNK),
        in_specs=[col(0), col(1), col(2), col(3),
                  pl.BlockSpec((RET_BATCH, 1, S), lambda b, c: (b, 0, 0))] + [const(a) for a in consts],
        out_specs=pl.BlockSpec(blk, lambda b, c: (b, c, 0)),
        out_shape=jax.ShapeDtypeStruct((B, S, RET_WIDTH), jnp.float32),
        scratch_shapes=[pltpu.VMEM((RET_BATCH, RET_HEADS, RET_HEAD_DIM, RET_HEAD_DIM), jnp.float32)],
        compiler_params=pltpu.CompilerParams(dimension_semantics=("arbitrary", "arbitrary")),
        name="retention",
    )(proj, proj, proj, proj, pos, *consts)


def _merge_kernel(cin_ref, halo_ref, ga_ref, gb_ref, ret_ref, x_ref, dw_ref, dwb_ref, cg_ref, cb_ref,
                  wc_ref, bc_ref, wo_ref, g1_ref, sc2_ref, sh2_ref, l1g_ref, l1b_ref,
                  x1_ref, h2_ref, y_buf, c_buf):
    def glu(a):
        a = a.astype(jnp.float32)
        return a[:, :CONV_WIDTH] * jax.nn.sigmoid(a[:, CONV_WIDTH:])

    first = pl.program_id(1) == 0
    halo = jnp.where(first, 0.0, glu(halo_ref[0]))
    main = glu(cin_ref[0])
    n_tiles = CONV_WIDTH // LANES
    for lt in range(n_tiles):
        lanes = slice(lt * LANES, (lt + 1) * LANES)
        y_buf[lt, 0:CONV_HALO, :] = halo[:, lanes]
        y_buf[lt, CONV_HALO:CONV_HALO + MERGE_TS, :] = main[:, lanes]

    shift = CONV_HALO - (CONV_TAPS - 1)

    def conv_rows(r, carry):
        row = r * (CONV_BLOCKS * SUBLANES)
        for lt in range(n_tiles):
            acc = [jnp.zeros((SUBLANES, LANES), jnp.float32) for _ in range(CONV_BLOCKS)]
            for j in range(CONV_TAPS):
                tap = jnp.broadcast_to(dw_ref[j, lt:lt + 1, :], (SUBLANES, LANES))
                for i in range(CONV_BLOCKS):
                    acc[i] = acc[i] + tap * y_buf[lt, pl.ds(row + i * SUBLANES + shift + j, SUBLANES), :]
            for i in range(CONV_BLOCKS):
                c_buf[lt, pl.ds(row + i * SUBLANES, SUBLANES), :] = acc[i]
        return carry

    lax.fori_loop(0, MERGE_TS // (CONV_BLOCKS * SUBLANES), conv_rows, 0)

    conv_dw_out = jnp.concatenate([c_buf[lt] for lt in range(n_tiles)], axis=1) + dwb_ref[...]
    c = _norm_rows(conv_dw_out) * cg_ref[...] + cb_ref[...]
    c = (c * jax.nn.sigmoid(c)).astype(jnp.bfloat16)
    conv = jnp.dot(c, wc_ref[...], preferred_element_type=jnp.float32) + bc_ref[...]
    merged = (jax.nn.sigmoid(ga_ref[0].astype(jnp.float32)) * ret_ref[0]
              + jax.nn.sigmoid(gb_ref[0].astype(jnp.float32)) * conv)
    mix = jnp.dot(merged.astype(jnp.bfloat16), wo_ref[...], preferred_element_type=jnp.float32)
    x1 = _norm_rows(DEEPNORM_ALPHA * x_ref[0] + g1_ref[0] * mix) * l1g_ref[...] + l1b_ref[...]
    x1_ref[0] = x1
    h2_ref[0] = _norm_rows(x1) * (1.0 + sc2_ref[0]) + sh2_ref[0]


def _merge(proj, ret, x, conv_dw, conv_dw_b, conv_ln_g, conv_ln_b, w_conv_out, b_conv_out, w_out,
           g1, sc2, sh2, ln1_g, ln1_b):
    B, S, D = x.shape
    glu_col = 4 * RET_WIDTH // (2 * CONV_WIDTH)
    gate_col = (4 * RET_WIDTH + 2 * CONV_WIDTH) // D
    halo_per_tile = MERGE_TS // CONV_HALO

    def tile(width, j):
        return pl.BlockSpec((1, MERGE_TS, width), lambda b, s: (b, s, j))

    def row(a):
        return pl.BlockSpec((1, a.shape[-1]), lambda b, s: (0, 0))

    def per_batch():
        return pl.BlockSpec((1, 1, D), lambda b, s: (b, 0, 0))

    def whole(a):
        return pl.BlockSpec(a.shape, lambda b, s: (0,) * a.ndim)

    rows = [conv_dw_b[None, :], conv_ln_g[None, :], conv_ln_b[None, :]]
    wc = w_conv_out.astype(jnp.bfloat16)
    dw3 = conv_dw.reshape(CONV_TAPS, CONV_WIDTH // LANES, LANES)
    wo = w_out.astype(jnp.bfloat16)
    out_spec = pl.BlockSpec((1, MERGE_TS, D), lambda b, s: (b, s, 0))
    return pl.pallas_call(
        _merge_kernel,
        grid=(B, S // MERGE_TS),
        in_specs=[
            tile(2 * CONV_WIDTH, glu_col),
            pl.BlockSpec((1, CONV_HALO, 2 * CONV_WIDTH),
                         lambda b, s: (b, jnp.maximum(s * halo_per_tile - 1, 0), glu_col)),
            tile(D, gate_col), tile(D, gate_col + 1), tile(D, 0), tile(D, 0),
            whole(dw3), row(rows[0]), row(rows[1]), row(rows[2]),
            whole(wc), row(b_conv_out[None, :]), whole(wo),
            per_batch(), per_batch(), per_batch(), row(ln1_g[None, :]), row(ln1_b[None, :]),
        ],
        out_specs=[out_spec, out_spec],
        out_shape=[jax.ShapeDtypeStruct((B, S, D), jnp.float32)] * 2,
        scratch_shapes=[
            pltpu.VMEM((CONV_WIDTH // LANES, CONV_HALO + MERGE_TS, LANES), jnp.float32),
            pltpu.VMEM((CONV_WIDTH // LANES, MERGE_TS, LANES), jnp.float32),
        ],
        compiler_params=pltpu.CompilerParams(dimension_semantics=("arbitrary", "arbitrary")),
        name="merge",
    )(proj, proj, proj, proj, ret, x, dw3, rows[0], rows[1], rows[2], wc, b_conv_out[None, :], wo,
      g1[:, None, :], sc2[:, None, :], sh2[:, None, :], ln1_g[None, :], ln1_b[None, :])


def _topk_rows(s, k):
    n_rows = s.shape[0]
    row = lax.broadcasted_iota(jnp.int32, s.shape, 0)
    vals, idxs = [], []
    for _ in range(k):
        m = jnp.max(s, axis=0, keepdims=True)
        i = jnp.min(jnp.where(s == m, row, n_rows), axis=0, keepdims=True)
        vals.append(m)
        idxs.append(i)
        s = jnp.where(row == i, -jnp.inf, s)
    return jnp.concatenate(vals, axis=0), jnp.concatenate(idxs, axis=0)


def _candidate_tiles():
    k = PEER_TOPK
    tiles = []
    for j in range(SUBLANES):
        n_i = k // (j + 1)
        for i0 in range(0, n_i, SUBLANES):
            tiles.append((i0, min(SUBLANES, n_i - i0), j, None))
    tiles.append((0, 1, None, SUBLANES))
    return tiles


def _router_kernel(h_ref, wqt_ref, keys_ref, idx_u_ref, idx_v_ref, gate_ref):
    k = PEER_TOPK
    hb = h_ref[...].astype(jnp.bfloat16)
    qt = lax.dot_general(wqt_ref[...], hb, (((1,), (1,)), ((), ())),
                         preferred_element_type=jnp.float32)
    sub = lax.broadcasted_iota(jnp.int32, (SUBLANES, TOK_BLOCK), 0)
    tiles = _candidate_tiles()
    tile_pos = []
    for i0, n_i, j, j0 in tiles:
        if j is not None:
            tile_pos.append(jnp.where(sub < n_i, (sub + i0) * k + j, k * k))
        else:
            tile_pos.append(sub + j0)
    all_e, all_gate = [], []
    for h in range(PEER_HEADS):
        tops = []
        for p in range(2):
            hp = 2 * h + p
            qhp = qt[hp * PEER_HALF:(hp + 1) * PEER_HALF, :].astype(jnp.bfloat16)
            s = jnp.dot(keys_ref[hp], qhp, preferred_element_type=jnp.float32)
            tops.append(_topk_rows(s, k))
        (s0, i0v), (s1, i1v) = tops
        e0 = i0v * N_KEYS
        cand_s, cand_e = [], []
        for (i0, n_i, j, j0), pos in zip(tiles, tile_pos):
            if j is not None:
                cs = s0[i0:i0 + SUBLANES, :] + s1[j:j + 1, :]
                ce = e0[i0:i0 + SUBLANES, :] + i1v[j:j + 1, :]
                cand_s.append(jnp.where(pos < k * k, cs, -jnp.inf))
            else:
                cs = s0[0:1, :] + s1[j0:j0 + SUBLANES, :]
                ce = e0[0:1, :] + i1v[j0:j0 + SUBLANES, :]
                cand_s.append(cs)
            cand_e.append(ce)
        cand_s = jnp.concatenate(cand_s, axis=0)
        cand_e = jnp.concatenate(cand_e, axis=0)
        cand_pos = jnp.concatenate(tile_pos, axis=0)
        best_s, best_e = [], []
        for _ in range(k):
            m = jnp.max(cand_s, axis=0, keepdims=True)
            pos = jnp.min(jnp.where(cand_s == m, cand_pos, k * k), axis=0, keepdims=True)
            hit = cand_pos == pos
            best_s.append(m)
            best_e.append(jnp.max(jnp.where(hit, cand_e, -1), axis=0, keepdims=True))
            cand_s = jnp.where(hit, -jnp.inf, cand_s)
        best_s = jnp.concatenate(best_s, axis=0)
        ex = jnp.exp(best_s - best_s[0:1, :])
        all_gate.append(ex / jnp.sum(ex, axis=0, keepdims=True))
        all_e.append(jnp.concatenate(best_e, axis=0))
    pair = lax.broadcasted_iota(jnp.int32, (PEER_PAIRS, TOK_BLOCK), 0)
    rows = jnp.concatenate(all_e, axis=0) * ROWS_PER_EXPERT + TAB_FRONT
    idx_u_ref[...] = rows - jnp.where((pair & 4) == 0, 0, ROWS_PER_EXPERT)
    idx_v_ref[...] = rows - jnp.where((pair & 1) == 0, 0, ROWS_PER_EXPERT)
    gate_ref[...] = jnp.concatenate(all_gate, axis=0).T


def _peer_router(h, wqt, keys):
    T = h.shape[0]
    return pl.pallas_call(
        _router_kernel,
        grid=(T // TOK_BLOCK,),
        in_specs=[
            pl.BlockSpec((TOK_BLOCK, D_MODEL), lambda i: (i, 0)),
            pl.BlockSpec(wqt.shape, lambda i: (0, 0)),
            pl.BlockSpec(keys.shape, lambda i: (0, 0, 0)),
        ],
        out_specs=[
            pl.BlockSpec((PEER_PAIRS, TOK_BLOCK), lambda i: (i, 0)),
            pl.BlockSpec((PEER_PAIRS, TOK_BLOCK), lambda i: (i, 0)),
            pl.BlockSpec((TOK_BLOCK, PEER_PAIRS), lambda i: (i, 0)),
        ],
        out_shape=[
            jax.ShapeDtypeStruct((T // TOK_BLOCK * PEER_PAIRS, TOK_BLOCK), jnp.int32),
            jax.ShapeDtypeStruct((T // TOK_BLOCK * PEER_PAIRS, TOK_BLOCK), jnp.int32),
            jax.ShapeDtypeStruct((T, PEER_PAIRS), jnp.float32),
        ],
        compiler_params=pltpu.CompilerParams(dimension_semantics=("arbitrary",)),
        name="peer_router",
    )(h, wqt, keys)


def _pack_kernel(t_ref, o_ref):
    i = pl.program_id(0)
    pad = jnp.logical_or(i == 0, i == pl.num_programs(0) - 1)
    x = jnp.where(pad, 0.0, t_ref[...]).astype(jnp.bfloat16).astype(jnp.float32)
    bits = lax.bitcast_convert_type(x, jnp.int32)
    half = D_MODEL // 2
    word = ((bits[:, :half] >> 16) & 0xFFFF) | bits[:, half:]
    for s_ in range(ROWS_PER_EXPERT):
        o_ref[pl.ds(s_, PACK_ROWS, stride=ROWS_PER_EXPERT), :] = word[:, s_ * LANES:(s_ + 1) * LANES]


def _pack_table(tab):
    n = tab.shape[0]
    steps = n // PACK_ROWS
    return pl.pallas_call(
        _pack_kernel,
        grid=(steps + 2,),
        in_specs=[pl.BlockSpec((PACK_ROWS, D_MODEL), lambda i: (jnp.clip(i - 1, 0, steps - 1), 0))],
        out_specs=pl.BlockSpec((TAB_FRONT, LANES), lambda i: (i, 0)),
        out_shape=jax.ShapeDtypeStruct(((steps + 2) * TAB_FRONT, LANES), jnp.int32),
        name="pack_table",
    )(tab)


def _unpack(c):
    lo = lax.bitcast_convert_type(c << 16, jnp.float32)
    hi = lax.bitcast_convert_type(c & HI_MASK, jnp.float32)
    return lo, hi


def _sublane_masks():
    sub = lax.broadcasted_iota(jnp.int32, (SUBLANES, LANES), 0)
    return sub < 4, (sub & 2) == 0, (sub & 1) == 0


def _idx_copy(idx_hbm, block, buf, sem):
    return pltpu.make_async_copy(idx_hbm.at[pl.ds(block * IDX_BLOCK, IDX_BLOCK)], buf, sem)


def _two_block_step(idx_hbm, idx_a, idx_b, sems, process):
    i = pl.program_id(0)

    @pl.when(i == 0)
    def _():
        _idx_copy(idx_hbm, 0, idx_a, sems.at[0]).start()

    _idx_copy(idx_hbm, 2 * i + 1, idx_b, sems.at[1]).start()
    _idx_copy(idx_hbm, 2 * i, idx_a, sems.at[0]).wait()
    process(idx_a, 0)

    @pl.when(i + 1 < pl.num_programs(0))
    def _():
        _idx_copy(idx_hbm, 2 * i + 2, idx_a, sems.at[0]).start()

    _idx_copy(idx_hbm, 2 * i + 1, idx_b, sems.at[1]).wait()
    process(idx_b, 1)


def _pair_rows(idx_buf, tok, p):
    return idx_buf.at[pl.ds(p * TOK_BLOCK, TOK_BLOCK)][tok]


def _peer_dots_kernel(idx_hbm, x_ref, gate_ref, tab_ref, w_ref, idx_a, idx_b, sems, q_buf, d_buf):
    m4, m2, m1 = _sublane_masks()
    lane = lax.broadcasted_iota(jnp.int32, (SUBLANES, TOK_BLOCK), 1)
    groups = [slice(g * SUBLANES, (g + 1) * SUBLANES) for g in range(PEER_PAIRS // SUBLANES)]

    def process(idx_buf, half):
        base = half * TOK_BLOCK
        q_buf[1] = jnp.zeros((PEER_PAIRS, LANES), jnp.float32)
        d_buf[...] = jnp.zeros((PEER_PAIRS, TOK_BLOCK), jnp.float32)

        def step(t, carry):
            for rows in groups:
                col = jnp.sum(q_buf[(t + 1) & 1, rows, :], axis=1, keepdims=True)
                d_buf[rows, :] = jnp.where(lane == t - 1, col, d_buf[rows, :])

            tok = jnp.minimum(t, TOK_BLOCK - 1)
            xt = x_ref[base + tok]
            xr = pltpu.roll(xt, 4, axis=0)
            xlo = jnp.where(m4, xt, xr)
            xhi = jnp.where(m4, xr, xt)

            def prod(pa):
                wa = tab_ref[pl.ds(_pair_rows(idx_buf, tok, pa), SUBLANES), :]
                wb = tab_ref[pl.ds(_pair_rows(idx_buf, tok, pa + 4), SUBLANES), :]
                lo, hi = _unpack(jnp.where(m4, wa, wb))
                return lo * xlo + hi * xhi

            for rows in groups:
                p = rows.start
                v = [prod(p + a) for a in (0, 2, 1, 3)]
                w = [jnp.where(m2, v1, pltpu.roll(v2, 2, axis=0)) + jnp.where(m2, pltpu.roll(v1, 6, axis=0), v2)
                     for v1, v2 in ((v[0], v[1]), (v[2], v[3]))]
                q_buf[t & 1, rows, :] = (jnp.where(m1, w[0], pltpu.roll(w[1], 1, axis=0))
                                         + jnp.where(m1, pltpu.roll(w[0], 7, axis=0), w[1]))
            return carry

        lax.fori_loop(0, TOK_BLOCK + 1, step, 0)

        dots = d_buf[...].T
        act = 0.5 * dots * (1.0 + lax.erf(dots * (2.0 ** -0.5)))
        out = pl.ds(base, TOK_BLOCK)
        w_ref[out, :] = act * gate_ref[out, :]

    _two_block_step(idx_hbm, idx_a, idx_b, sems, process)


def _peer_scratch():
    return [
        pltpu.SMEM((IDX_BLOCK,), jnp.int32),
        pltpu.SMEM((IDX_BLOCK,), jnp.int32),
        pltpu.SemaphoreType.DMA((2,)),
    ]


def _peer_dots(idx, x, gate, tab):
    T = x.shape[0]
    return pl.pallas_call(
        _peer_dots_kernel,
        grid=(T // (2 * TOK_BLOCK),),
        in_specs=[
            pl.BlockSpec(memory_space=pl.ANY),
            pl.BlockSpec((2 * TOK_BLOCK, SUBLANES, LANES), lambda i: (i, 0, 0)),
            pl.BlockSpec((2 * TOK_BLOCK, PEER_PAIRS), lambda i: (i, 0)),
            pl.BlockSpec(memory_space=pltpu.VMEM),
        ],
        out_specs=pl.BlockSpec((2 * TOK_BLOCK, PEER_PAIRS), lambda i: (i, 0)),
        out_shape=jax.ShapeDtypeStruct((T, PEER_PAIRS), jnp.float32),
        scratch_shapes=_peer_scratch() + [
            pltpu.VMEM((2, PEER_PAIRS, LANES), jnp.float32),
            pltpu.VMEM((PEER_PAIRS, TOK_BLOCK), jnp.float32),
        ],
        compiler_params=pltpu.CompilerParams(
            dimension_semantics=("arbitrary",), vmem_limit_bytes=PEER_VMEM_LIMIT),
        name="peer_dots",
    )(idx.reshape(-1), x.reshape(T, SUBLANES, LANES), gate, tab)


def _peer_mix_kernel(idx_hbm, w_ref, tab_ref, x_ref, g2_ref, gain_ref, bias_ref, out_ref,
                     idx_a, idx_b, sems, m_buf, o_buf, f_buf):
    m4, _, _ = _sublane_masks()
    n_slots = 2 * MIX_UNROLL

    def process(idx_buf, half):
        base = half * TOK_BLOCK

        def spread_weights(tok, slot):
            m = jnp.broadcast_to(w_ref[pl.ds(base + tok, 1), :], (PEER_PAIRS, PEER_PAIRS)).T
            for g in range(PEER_PAIRS // SUBLANES):
                rows = m[g * SUBLANES:(g + 1) * SUBLANES, :]
                for r in range(ROWS_PER_EXPERT):
                    m_buf[slot, pl.ds(g * SUBLANES * ROWS_PER_EXPERT + r, SUBLANES, stride=ROWS_PER_EXPERT), :] = rows

        for tok in range(n_slots):
            spread_weights(tok, tok)

        def token(t, slot, u):
            acc_lo = jnp.zeros((SUBLANES, LANES), jnp.float32)
            acc_hi = jnp.zeros((SUBLANES, LANES), jnp.float32)
            for j in range(PEER_PAIRS // 2):
                wa = tab_ref[pl.ds(_pair_rows(idx_buf, t, 2 * j), SUBLANES), :]
                wb = tab_ref[pl.ds(_pair_rows(idx_buf, t, 2 * j + 1), SUBLANES), :]
                lo, hi = _unpack(jnp.where(m4, wa, wb))
                wt = m_buf[slot, j * SUBLANES:(j + 1) * SUBLANES, :]
                acc_lo = acc_lo + wt * lo
                acc_hi = acc_hi + wt * hi
            lo = acc_lo + pltpu.roll(acc_lo, 4, axis=0)
            hi = acc_hi + pltpu.roll(acc_hi, 4, axis=0)
            o_buf[pl.ds(pl.multiple_of(u * SUBLANES, SUBLANES), SUBLANES), :] = jnp.where(m4, lo, hi)

        def tokens(k, i):
            t0 = k * n_slots
            for grp in range(2):
                slots = range(grp * MIX_UNROLL, (grp + 1) * MIX_UNROLL)
                for s_ in slots:
                    token(t0 + s_, s_, i * n_slots + s_)
                for s_ in slots:
                    spread_weights(jnp.minimum(t0 + n_slots + s_, TOK_BLOCK - 1), s_)

        trips = SUBLANES // n_slots

        def eight_tokens(g8, carry):
            def trip(i, c):
                tokens(g8 * trips + i, i)
                return c

            lax.fori_loop(0, trips, trip, 0)
            row0 = pl.multiple_of(base + g8 * SUBLANES, SUBLANES)
            for r in range(SUBLANES):
                f_buf[pl.ds(row0, SUBLANES), r * LANES:(r + 1) * LANES] = o_buf[pl.ds(r, SUBLANES, stride=SUBLANES), :]
            return carry

        lax.fori_loop(0, TOK_BLOCK // SUBLANES, eight_tokens, 0)

    _two_block_step(idx_hbm, idx_a, idx_b, sems, process)
    y = DEEPNORM_ALPHA * x_ref[...] + g2_ref[0] * f_buf[...]
    out_ref[...] = _norm_rows(y) * gain_ref[...] + bias_ref[...]


def _peer_mix(idx, w, tab, x, g2, gain, bias):
    T, D = x.shape
    blk = 2 * TOK_BLOCK
    blocks_per_seq = T // g2.shape[0] // blk
    return pl.pallas_call(
        _peer_mix_kernel,
        grid=(T // blk,),
        in_specs=[
            pl.BlockSpec(memory_space=pl.ANY),
            pl.BlockSpec((blk, PEER_PAIRS), lambda i: (i, 0)),
            pl.BlockSpec(memory_space=pltpu.VMEM),
            pl.BlockSpec((blk, D), lambda i: (i, 0)),
            pl.BlockSpec((1, 1, D), lambda i: (i // blocks_per_seq, 0, 0)),
            pl.BlockSpec((1, D), lambda i: (0, 0)),
            pl.BlockSpec((1, D), lambda i: (0, 0)),
        ],
        out_specs=pl.BlockSpec((blk, D), lambda i: (i, 0)),
        out_shape=jax.ShapeDtypeStruct((T, D), jnp.float32),
        scratch_shapes=_peer_scratch() + [
            pltpu.VMEM((2 * MIX_UNROLL, ROWS_PER_EXPERT * PEER_PAIRS, LANES), jnp.float32),
            pltpu.VMEM((SUBLANES * SUBLANES, LANES), jnp.float32),
            pltpu.VMEM((blk, D), jnp.float32),
        ],
        compiler_params=pltpu.CompilerParams(
            dimension_semantics=("arbitrary",), vmem_limit_bytes=PEER_VMEM_LIMIT),
        name="peer_mix",
    )(idx.reshape(-1), w, tab, x, g2[:, None, :], gain[None, :], bias[None, :])


def _peer(h, x, g2, gain, bias, wq, subkeys, u_tab, v_tab):
    B, S, D = h.shape
    hf = h.reshape(B * S, D)
    wqt = wq.T.astype(jnp.bfloat16)
    keys = subkeys.reshape(PEER_HEADS * 2, N_KEYS, PEER_HALF).astype(jnp.bfloat16)
    idx_u, idx_v, gate = _peer_router(hf, wqt, keys)
    w = _peer_dots(idx_u, hf, gate, _pack_table(u_tab))
    return _peer_mix(idx_v, w, _pack_table(v_tab), x.reshape(B * S, D), g2, gain, bias).reshape(B, S, D)


def kernel(x, c, positions, w_ada, b_ada, w_in, conv_dw, conv_dw_b, conv_ln_g, conv_ln_b, w_conv_out, b_conv_out, w_out, ln1_g, ln1_b, peer_wq, peer_subkeys, peer_u, peer_v, ln2_g, ln2_b):
    l = 0
    mod = _adaln(c, w_ada[l], b_ada[l])
    sh1, sc1, g1, sh2, sc2, g2 = jnp.split(mod, 6, axis=-1)
    proj = _input_proj(x, sc1, sh1, w_in[l].astype(jnp.bfloat16))
    ret = _retention(proj, positions.astype(jnp.float32)[:, None, :], _retention_consts())
    x1, h2 = _merge(proj, ret, x, conv_dw[l], conv_dw_b[l], conv_ln_g[l], conv_ln_b[l],
                    w_conv_out[l], b_conv_out[l], w_out[l], g1, sc2, sh2, ln1_g[l], ln1_b[l])
    return _peer(h2, x1, g2, ln2_g[l], ln2_b[l], peer_wq[l], peer_subkeys[l], peer_u[l], peer_v[l])
```

```python
import jax
import jax.numpy as jnp
from jax import lax
from jax.experimental import pallas as pl
from jax.experimental.pallas import tpu as pltpu

D_MODEL = 1024
CHUNK = 64
RET_HEADS = 4
RET_HEAD_DIM = 256
RET_WIDTH = RET_HEADS * RET_HEAD_DIM
ROPE_THETA = 10000.0
CONV_WIDTH = D_MODEL
CONV_TAPS = 31
PEER_HEADS = 8
PEER_HALF = 128
N_KEYS = 128
N_EXPERTS = N_KEYS * N_KEYS
PEER_TOPK = 16
PEER_PAIRS = PEER_HEADS * PEER_TOPK
DEPTH = 1
DEEPNORM_ALPHA = (2.0 * DEPTH) ** 0.25
LN_EPS = 1e-5

SUBLANES = 8
LANES = 128
ROWS_PER_EXPERT = D_MODEL // (2 * LANES)
PACK_ROWS = 256
TAB_FRONT = PACK_ROWS * ROWS_PER_EXPERT
HI_MASK = -65536
TOK_BLOCK = 128
IDX_BLOCK = PEER_PAIRS * TOK_BLOCK
MIX_UNROLL = 1
PEER_VMEM_LIMIT = 52 * 1024 * 1024
PROJ_TM = 1024
PROJ_TN = 4096
RET_BATCH = 4
MERGE_TS = 512
CONV_HALO = 32
CONV_BLOCKS = 4


def _adaln_kernel(c_ref, w_ref, b_ref, o_ref):
    c = c_ref[...]
    cond = (c * jax.nn.sigmoid(c)).astype(jnp.bfloat16)
    o_ref[...] = jnp.dot(cond, w_ref[...].astype(jnp.bfloat16),
                         preferred_element_type=jnp.float32) + b_ref[...]


def _adaln(c, w, b):
    B = c.shape[0]
    n = w.shape[1]
    return pl.pallas_call(
        _adaln_kernel,
        grid=(n // D_MODEL,),
        in_specs=[
            pl.BlockSpec((B, D_MODEL), lambda j: (0, 0)),
            pl.BlockSpec((D_MODEL, D_MODEL), lambda j: (0, j)),
            pl.BlockSpec((1, D_MODEL), lambda j: (0, j)),
        ],
        out_specs=pl.BlockSpec((B, D_MODEL), lambda j: (0, j)),
        out_shape=jax.ShapeDtypeStruct((B, n), jnp.float32),
        name="adaln",
    )(c, w, b[None, :])


def _norm_rows(x):
    mu = jnp.mean(x, axis=-1, keepdims=True)
    xc = x - mu
    var = jnp.mean(xc * xc, axis=-1, keepdims=True)
    return xc * lax.rsqrt(var + LN_EPS)


def _proj_kernel(x_ref, sc_ref, sh_ref, w_ref, o_ref, h_ref):
    @pl.when(pl.program_id(2) == 0)
    def _():
        h_ref[...] = (_norm_rows(x_ref[0]) * (1.0 + sc_ref[0]) + sh_ref[0]).astype(jnp.bfloat16)

    o_ref[0] = jnp.dot(h_ref[...], w_ref[...], preferred_element_type=jnp.float32).astype(o_ref.dtype)


def _input_proj(x, sc, sh, w):
    B, S, D = x.shape
    n = w.shape[1]
    return pl.pallas_call(
        _proj_kernel,
        grid=(B, S // PROJ_TM, n // PROJ_TN),
        in_specs=[
            pl.BlockSpec((1, PROJ_TM, D), lambda b, i, j: (b, i, 0)),
            pl.BlockSpec((1, 1, D), lambda b, i, j: (b, 0, 0)),
            pl.BlockSpec((1, 1, D), lambda b, i, j: (b, 0, 0)),
            pl.BlockSpec((D, PROJ_TN), lambda b, i, j: (0, j)),
        ],
        out_specs=pl.BlockSpec((1, PROJ_TM, PROJ_TN), lambda b, i, j: (b, i, j)),
        out_shape=jax.ShapeDtypeStruct((B, S, n), jnp.bfloat16),
        scratch_shapes=[pltpu.VMEM((PROJ_TM, D), jnp.bfloat16)],
        compiler_params=pltpu.CompilerParams(
            dimension_semantics=("arbitrary", "arbitrary", "arbitrary")),
        name="input_proj",
    )(x, sc[:, None, :], sh[:, None, :], w)


def _retention_consts():
    h = jnp.arange(RET_HEADS, dtype=jnp.float32)
    log_gamma = jnp.log(1.0 - 2.0 ** (-5.0 - h))
    pos = jnp.arange(CHUNK, dtype=jnp.float32)
    rel = jnp.abs(pos[:, None] - pos[None, :])
    intra = jnp.exp(log_gamma[:, None, None] * rel)
    q_decay = jnp.exp(log_gamma[:, None, None] * (pos[None, :, None] + 1.0))
    k_decay = jnp.exp(log_gamma[:, None, None] * (CHUNK - 1.0 - pos[None, :, None]))
    chunk_decay = jnp.broadcast_to(jnp.exp(log_gamma * CHUNK)[:, None, None], (RET_HEADS, 1, LANES))
    half = RET_HEAD_DIM // 2
    inv_freq = ROPE_THETA ** (-jnp.arange(half, dtype=jnp.float32) / half)
    return inv_freq[None, :], intra, q_decay, k_decay, chunk_decay


def _retention_kernel(q_ref, k_ref, v_ref, g_ref, pos_ref, invf_ref, intra_ref, qd_ref, kd_ref, cd_ref,
                      o_ref, state_ref):
    @pl.when(pl.program_id(1) == 0)
    def _():
        state_ref[...] = jnp.zeros(state_ref.shape, jnp.float32)

    half = RET_HEAD_DIM // 2
    chunk = pl.program_id(1)
    lane_start = pl.multiple_of((chunk * CHUNK // LANES) * LANES, LANES)
    pick = (lax.broadcasted_iota(jnp.int32, (CHUNK, LANES), 1)
            == lax.broadcasted_iota(jnp.int32, (CHUNK, LANES), 0) + (chunk * CHUNK) % LANES)

    def rotate(t, cos, sin):
        t1, t2 = t[:, :half], t[:, half:]
        return jnp.concatenate([t1 * cos - t2 * sin, t1 * sin + t2 * cos], axis=1)

    for b in range(RET_BATCH):
        pos_row = pos_ref[b, :, pl.ds(lane_start, LANES)]
        pos_col = jnp.sum(jnp.where(pick, pos_row, 0.0), axis=1, keepdims=True)
        ang = pos_col * invf_ref[...]
        cos, sin = jnp.cos(ang), jnp.sin(ang)
        for h in range(RET_HEADS):
            cols = slice(h * RET_HEAD_DIM, (h + 1) * RET_HEAD_DIM)
            q = rotate(q_ref[b, :, cols].astype(jnp.float32), cos, sin).astype(jnp.bfloat16)
            k = rotate(k_ref[b, :, cols].astype(jnp.float32), cos, sin) * (RET_HEAD_DIM ** -0.5)
            v = v_ref[b, :, cols]
            s = lax.dot_general(q, k.astype(jnp.bfloat16), (((1,), (1,)), ((), ())),
                                preferred_element_type=jnp.float32) * intra_ref[h]
            intra = jnp.dot(s.astype(jnp.bfloat16), v, preferred_element_type=jnp.float32)
            state = state_ref[b, h]
            cross = jnp.dot(q, state.astype(jnp.bfloat16), preferred_element_type=jnp.float32) * qd_ref[h]
            kd = (k * kd_ref[h]).astype(jnp.bfloat16)
            state_ref[b, h] = state * cd_ref[h][:, :1] + lax.dot_general(
                kd, v, (((0,), (0,)), ((), ())), preferred_element_type=jnp.float32)
            g = g_ref[b, :, cols].astype(jnp.float32)
            o_ref[b, :, cols] = _norm_rows(intra + cross) * (g * jax.nn.sigmoid(g))


def _retention(proj, pos, consts):
    B, S, _ = proj.shape
    blk = (RET_BATCH, CHUNK, RET_WIDTH)

    def col(j):
        return pl.BlockSpec(blk, lambda b, c: (b, c, j))

    def const(a):
        return pl.BlockSpec(a.shape, lambda b, c: (0,) * a.ndim)

    return pl.pallas_call(
        _retention_kernel,
        grid=(B // RET_BATCH, S // CHUNK),
        in_specs=[col(0), col(1), col(2), col(3),
                  pl.BlockSpec((RET_BATCH, 1, S), lambda b, c: (b, 0, 0))] + [const(a) for a in consts],
        out_specs=pl.BlockSpec(blk, lambda b, c: (b, c, 0)),
        out_shape=jax.ShapeDtypeStruct((B, S, RET_WIDTH), jnp.float32),
        scratch_shapes=[pltpu.VMEM((RET_BATCH, RET_HEADS, RET_HEAD_DIM, RET_HEAD_DIM), jnp.float32)],
        compiler_params=pltpu.CompilerParams(dimension_semantics=("arbitrary", "arbitrary")),
        name="retention",
    )(proj, proj, proj, proj, pos, *consts)


def _merge_kernel(cin_ref, halo_ref, ga_ref, gb_ref, ret_ref, x_ref, dw_ref, dwb_ref, cg_ref, cb_ref,
                  wc_ref, bc_ref, wo_ref, g1_ref, sc2_ref, sh2_ref, l1g_ref, l1b_ref,
                  x1_ref, h2_ref, y_buf, c_buf):
    def glu(a):
        a = a.astype(jnp.float32)
        return a[:, :CONV_WIDTH] * jax.nn.sigmoid(a[:, CONV_WIDTH:])

    first = pl.program_id(1) == 0
    halo = jnp.where(first, 0.0, glu(halo_ref[0]))
    main = glu(cin_ref[0])
    n_tiles = CONV_WIDTH // LANES
    for lt in range(n_tiles):
        lanes = slice(lt * LANES, (lt + 1) * LANES)
        y_buf[lt, 0:CONV_HALO, :] = halo[:, lanes]
        y_buf[lt, CONV_HALO:CONV_HALO + MERGE_TS, :] = main[:, lanes]

    shift = CONV_HALO - (CONV_TAPS - 1)

    def conv_rows(r, carry):
        row = r * (CONV_BLOCKS * SUBLANES)
        for lt in range(n_tiles):
            acc = [jnp.zeros((SUBLANES, LANES), jnp.float32) for _ in range(CONV_BLOCKS)]
            for j in range(CONV_TAPS):
                tap = jnp.broadcast_to(dw_ref[j, lt:lt + 1, :], (SUBLANES, LANES))
                for i in range(CONV_BLOCKS):
                    acc[i] = acc[i] + tap * y_buf[lt, pl.ds(row + i * SUBLANES + shift + j, SUBLANES), :]
            for i in range(CONV_BLOCKS):
                c_buf[lt, pl.ds(row + i * SUBLANES, SUBLANES), :] = acc[i]
        return carry

    lax.fori_loop(0, MERGE_TS // (CONV_BLOCKS * SUBLANES), conv_rows, 0)

    conv_dw_out = jnp.concatenate([c_buf[lt] for lt in range(n_tiles)], axis=1) + dwb_ref[...]
    c = _norm_rows(conv_dw_out) * cg_ref[...] + cb_ref[...]
    c = (c * jax.nn.sigmoid(c)).astype(jnp.bfloat16)
    conv = jnp.dot(c, wc_ref[...], preferred_element_type=jnp.float32) + bc_ref[...]
    merged = (jax.nn.sigmoid(ga_ref[0].astype(jnp.float32)) * ret_ref[0]
              + jax.nn.sigmoid(gb_ref[0].astype(jnp.float32)) * conv)
    mix = jnp.dot(merged.astype(jnp.bfloat16), wo_ref[...], preferred_element_type=jnp.float32)
    x1 = _norm_rows(DEEPNORM_ALPHA * x_ref[0] + g1_ref[0] * mix) * l1g_ref[...] + l1b_ref[...]
    x1_ref[0] = x1
    h2_ref[0] = _norm_rows(x1) * (1.0 + sc2_ref[0]) + sh2_ref[0]


def _merge(proj, ret, x, conv_dw, conv_dw_b, conv_ln_g, conv_ln_b, w_conv_out, b_conv_out, w_out,
           g1, sc2, sh2, ln1_g, ln1_b):
    B, S, D = x.shape
    glu_col = 4 * RET_WIDTH // (2 * CONV_WIDTH)
    gate_col = (4 * RET_WIDTH + 2 * CONV_WIDTH) // D
    halo_per_tile = MERGE_TS // CONV_HALO

    def tile(width, j):
        return pl.BlockSpec((1, MERGE_TS, width), lambda b, s: (b, s, j))

    def row(a):
        return pl.BlockSpec((1, a.shape[-1]), lambda b, s: (0, 0))

    def per_batch():
        return pl.BlockSpec((1, 1, D), lambda b, s: (b, 0, 0))

    def whole(a):
        return pl.BlockSpec(a.shape, lambda b, s: (0,) * a.ndim)

    rows = [conv_dw_b[None, :], conv_ln_g[None, :], conv_ln_b[None, :]]
    wc = w_conv_out.astype(jnp.bfloat16)
    dw3 = conv_dw.reshape(CONV_TAPS, CONV_WIDTH // LANES, LANES)
    wo = w_out.astype(jnp.bfloat16)
    out_spec = pl.BlockSpec((1, MERGE_TS, D), lambda b, s: (b, s, 0))
    return pl.pallas_call(
        _merge_kernel,
        grid=(B, S // MERGE_TS),
        in_specs=[
            tile(2 * CONV_WIDTH, glu_col),
            pl.BlockSpec((1, CONV_HALO, 2 * CONV_WIDTH),
                         lambda b, s: (b, jnp.maximum(s * halo_per_tile - 1, 0), glu_col)),
            tile(D, gate_col), tile(D, gate_col + 1), tile(D, 0), tile(D, 0),
            whole(dw3), row(rows[0]), row(rows[1]), row(rows[2]),
            whole(wc), row(b_conv_out[None, :]), whole(wo),
            per_batch(), per_batch(), per_batch(), row(ln1_g[None, :]), row(ln1_b[None, :]),
        ],
        out_specs=[out_spec, out_spec],
        out_shape=[jax.ShapeDtypeStruct((B, S, D), jnp.float32)] * 2,
        scratch_shapes=[
            pltpu.VMEM((CONV_WIDTH // LANES, CONV_HALO + MERGE_TS, LANES), jnp.float32),
            pltpu.VMEM((CONV_WIDTH // LANES, MERGE_TS, LANES), jnp.float32),
        ],
        compiler_params=pltpu.CompilerParams(dimension_semantics=("arbitrary", "arbitrary")),
        name="merge",
    )(proj, proj, proj, proj, ret, x, dw3, rows[0], rows[1], rows[2], wc, b_conv_out[None, :], wo,
      g1[:, None, :], sc2[:, None, :], sh2[:, None, :], ln1_g[None, :], ln1_b[None, :])


def _topk_rows(s, k):
    n_rows = s.shape[0]
    sub = lax.broadcasted_iota(jnp.int32, (SUBLANES, s.shape[1]), 0)
    blocks = [s[b * SUBLANES:(b + 1) * SUBLANES, :] for b in range(n_rows // SUBLANES)]
    vals, idxs = [], []
    for _ in range(k):
        level = [(blk, b) for b, blk in enumerate(blocks)]
        while len(level) > 1:
            nxt = []
            for (va, ia), (vb, ib) in zip(level[0::2], level[1::2]):
                nxt.append((jnp.maximum(va, vb), jnp.where(vb > va, ib, ia)))
            level = nxt
        v8, b8 = level[0]
        m = jnp.max(v8, axis=0, keepdims=True)
        i = jnp.min(jnp.where(v8 == m, b8 * SUBLANES + sub, n_rows), axis=0, keepdims=True)
        vals.append(m)
        idxs.append(i)
        blocks = [jnp.where(sub + b * SUBLANES == i, -jnp.inf, blk) for b, blk in enumerate(blocks)]
    return jnp.concatenate(vals, axis=0), jnp.concatenate(idxs, axis=0)


def _candidate_tiles():
    k = PEER_TOPK
    tiles = []
    for j in range(SUBLANES):
        n_i = k // (j + 1)
        for i0 in range(0, n_i, SUBLANES):
            tiles.append((i0, min(SUBLANES, n_i - i0), j, None))
    tiles.append((0, 1, None, SUBLANES))
    return tiles


def _router_kernel(h_ref, wqt_ref, keys_ref, idx_u_ref, idx_v_ref, gate_ref):
    k = PEER_TOPK
    hb = h_ref[...].astype(jnp.bfloat16)
    qt = lax.dot_general(wqt_ref[...], hb, (((1,), (1,)), ((), ())),
                         preferred_element_type=jnp.float32)
    sub = lax.broadcasted_iota(jnp.int32, (SUBLANES, TOK_BLOCK), 0)
    tiles = _candidate_tiles()
    tile_pos = []
    for i0, n_i, j, j0 in tiles:
        if j is not None:
            tile_pos.append(jnp.where(sub < n_i, (sub + i0) * k + j, k * k))
        else:
            tile_pos.append(sub + j0)
    all_e, all_gate = [], []
    for h in range(PEER_HEADS):
        tops = []
        for p in range(2):
            hp = 2 * h + p
            qhp = qt[hp * PEER_HALF:(hp + 1) * PEER_HALF, :].astype(jnp.bfloat16)
            s = jnp.dot(keys_ref[hp], qhp, preferred_element_type=jnp.float32)
            tops.append(_topk_rows(s, k))
        (s0, i0v), (s1, i1v) = tops
        e0 = i0v * N_KEYS
        cand_s, cand_e = [], []
        for (i0, n_i, j, j0), pos in zip(tiles, tile_pos):
            if j is not None:
                cs = s0[i0:i0 + SUBLANES, :] + s1[j:j + 1, :]
                ce = e0[i0:i0 + SUBLANES, :] + i1v[j:j + 1, :]
                cand_s.append(jnp.where(pos < k * k, cs, -jnp.inf))
            else:
                cs = s0[0:1, :] + s1[j0:j0 + SUBLANES, :]
                ce = e0[0:1, :] + i1v[j0:j0 + SUBLANES, :]
                cand_s.append(cs)
            cand_e.append(ce)
        cand_s = jnp.concatenate(cand_s, axis=0)
        cand_e = jnp.concatenate(cand_e, axis=0)
        cand_pos = jnp.concatenate(tile_pos, axis=0)
        best_s, best_e = [], []
        for _ in range(k):
            m = jnp.max(cand_s, axis=0, keepdims=True)
            pos = jnp.min(jnp.where(cand_s == m, cand_pos, k * k), axis=0, keepdims=True)
            hit = cand_pos == pos
            best_s.append(m)
            best_e.append(jnp.max(jnp.where(hit, cand_e, -1), axis=0, keepdims=True))
            cand_s = jnp.where(hit, -jnp.inf, cand_s)
        best_s = jnp.concatenate(best_s, axis=0)
        ex = jnp.exp(best_s - best_s[0:1, :])
        all_gate.append(ex / jnp.sum(ex, axis=0, keepdims=True))
        all_e.append(jnp.concatenate(best_e, axis=0))
    pair = lax.broadcasted_iota(jnp.int32, (PEER_PAIRS, TOK_BLOCK), 0)
    rows = jnp.concatenate(all_e, axis=0) * ROWS_PER_EXPERT + TAB_FRONT
    idx_u_ref[...] = rows - jnp.where((pair & 4) == 0, 0, ROWS_PER_EXPERT)
    idx_v_ref[...] = rows - jnp.where((pair & 1) == 0, 0, ROWS_PER_EXPERT)
    gate_ref[...] = jnp.concatenate(all_gate, axis=0).T


def _peer_router(h, wqt, keys):
    T = h.shape[0]
    return pl.pallas_call(
        _router_kernel,
        grid=(T // TOK_BLOCK,),
        in_specs=[
            pl.BlockSpec((TOK_BLOCK, D_MODEL), lambda i: (i, 0)),
            pl.BlockSpec(wqt.shape, lambda i: (0, 0)),
            pl.BlockSpec(keys.shape, lambda i: (0, 0, 0)),
        ],
        out_specs=[
            pl.BlockSpec((PEER_PAIRS, TOK_BLOCK), lambda i: (i, 0)),
            pl.BlockSpec((PEER_PAIRS, TOK_BLOCK), lambda i: (i, 0)),
            pl.BlockSpec((TOK_BLOCK, PEER_PAIRS), lambda i: (i, 0)),
        ],
        out_shape=[
            jax.ShapeDtypeStruct((T // TOK_BLOCK * PEER_PAIRS, TOK_BLOCK), jnp.int32),
            jax.ShapeDtypeStruct((T // TOK_BLOCK * PEER_PAIRS, TOK_BLOCK), jnp.int32),
            jax.ShapeDtypeStruct((T, PEER_PAIRS), jnp.float32),
        ],
        compiler_params=pltpu.CompilerParams(dimension_semantics=("arbitrary",)),
        name="peer_router",
    )(h, wqt, keys)


def _pack_kernel(t_ref, o_ref):
    i = pl.program_id(0)
    pad = jnp.logical_or(i == 0, i == pl.num_programs(0) - 1)
    x = jnp.where(pad, 0.0, t_ref[...]).astype(jnp.bfloat16).astype(jnp.float32)
    bits = lax.bitcast_convert_type(x, jnp.int32)
    half = D_MODEL // 2
    word = ((bits[:, :half] >> 16) & 0xFFFF) | bits[:, half:]
    for s_ in range(ROWS_PER_EXPERT):
        o_ref[pl.ds(s_, PACK_ROWS, stride=ROWS_PER_EXPERT), :] = word[:, s_ * LANES:(s_ + 1) * LANES]


def _pack_table(tab):
    n = tab.shape[0]
    steps = n // PACK_ROWS
    return pl.pallas_call(
        _pack_kernel,
        grid=(steps + 2,),
        in_specs=[pl.BlockSpec((PACK_ROWS, D_MODEL), lambda i: (jnp.clip(i - 1, 0, steps - 1), 0))],
        out_specs=pl.BlockSpec((TAB_FRONT, LANES), lambda i: (i, 0)),
        out_shape=jax.ShapeDtypeStruct(((steps + 2) * TAB_FRONT, LANES), jnp.int32),
        name="pack_table",
    )(tab)


def _unpack(c):
    lo = lax.bitcast_convert_type(c << 16, jnp.float32)
    hi = lax.bitcast_convert_type(c & HI_MASK, jnp.float32)
    return lo, hi


def _sublane_masks():
    sub = lax.broadcasted_iota(jnp.int32, (SUBLANES, LANES), 0)
    return sub < 4, (sub & 2) == 0, (sub & 1) == 0


def _idx_copy(idx_hbm, block, buf, sem):
    return pltpu.make_async_copy(idx_hbm.at[pl.ds(block * IDX_BLOCK, IDX_BLOCK)], buf, sem)


def _two_block_step(idx_hbm, idx_a, idx_b, sems, process):
    i = pl.program_id(0)

    @pl.when(i == 0)
    def _():
        _idx_copy(idx_hbm, 0, idx_a, sems.at[0]).start()

    _idx_copy(idx_hbm, 2 * i + 1, idx_b, sems.at[1]).start()
    _idx_copy(idx_hbm, 2 * i, idx_a, sems.at[0]).wait()
    process(idx_a, 0)

    @pl.when(i + 1 < pl.num_programs(0))
    def _():
        _idx_copy(idx_hbm, 2 * i + 2, idx_a, sems.at[0]).start()

    _idx_copy(idx_hbm, 2 * i + 1, idx_b, sems.at[1]).wait()
    process(idx_b, 1)


def _pair_rows(idx_buf, tok, p):
    return idx_buf.at[pl.ds(p * TOK_BLOCK, TOK_BLOCK)][tok]


def _peer_dots_kernel(idx_hbm, x_ref, gate_ref, tab_ref, w_ref, idx_a, idx_b, sems, q_buf, d_buf):
    m4, m2, m1 = _sublane_masks()
    lane = lax.broadcasted_iota(jnp.int32, (SUBLANES, TOK_BLOCK), 1)
    groups = [slice(g * SUBLANES, (g + 1) * SUBLANES) for g in range(PEER_PAIRS // SUBLANES)]

    def process(idx_buf, half):
        base = half * TOK_BLOCK
        q_buf[1] = jnp.zeros((PEER_PAIRS, LANES), jnp.float32)
        d_buf[...] = jnp.zeros((PEER_PAIRS, TOK_BLOCK), jnp.float32)

        def step(t, carry):
            for rows in groups:
                col = jnp.sum(q_buf[(t + 1) & 1, rows, :], axis=1, keepdims=True)
                d_buf[rows, :] = jnp.where(lane == t - 1, col, d_buf[rows, :])

            tok = jnp.minimum(t, TOK_BLOCK - 1)
            xt = x_ref[base + tok]
            xr = pltpu.roll(xt, 4, axis=0)
            xlo = jnp.where(m4, xt, xr)
            xhi = jnp.where(m4, xr, xt)

            def prod(pa):
                wa = tab_ref[pl.ds(_pair_rows(idx_buf, tok, pa), SUBLANES), :]
                wb = tab_ref[pl.ds(_pair_rows(idx_buf, tok, pa + 4), SUBLANES), :]
                lo, hi = _unpack(jnp.where(m4, wa, wb))
                return lo * xlo + hi * xhi

            for rows in groups:
                p = rows.start
                v = [prod(p + a) for a in (0, 2, 1, 3)]
                w = [jnp.where(m2, v1, pltpu.roll(v2, 2, axis=0)) + jnp.where(m2, pltpu.roll(v1, 6, axis=0), v2)
                     for v1, v2 in ((v[0], v[1]), (v[2], v[3]))]
                q_buf[t & 1, rows, :] = (jnp.where(m1, w[0], pltpu.roll(w[1], 1, axis=0))
                                         + jnp.where(m1, pltpu.roll(w[0], 7, axis=0), w[1]))
            return carry

        lax.fori_loop(0, TOK_BLOCK + 1, step, 0)

        dots = d_buf[...].T
        act = 0.5 * dots * (1.0 + lax.erf(dots * (2.0 ** -0.5)))
        out = pl.ds(base, TOK_BLOCK)
        w_ref[out, :] = act * gate_ref[out, :]

    _two_block_step(idx_hbm, idx_a, idx_b, sems, process)


def _peer_scratch():
    return [
        pltpu.SMEM((IDX_BLOCK,), jnp.int32),
        pltpu.SMEM((IDX_BLOCK,), jnp.int32),
        pltpu.SemaphoreType.DMA((2,)),
    ]


def _peer_dots(idx, x, gate, tab):
    T = x.shape[0]
    return pl.pallas_call(
        _peer_dots_kernel,
        grid=(T // (2 * TOK_BLOCK),),
        in_specs=[
            pl.BlockSpec(memory_space=pl.ANY),
            pl.BlockSpec((2 * TOK_BLOCK, SUBLANES, LANES), lambda i: (i, 0, 0)),
            pl.BlockSpec((2 * TOK_BLOCK, PEER_PAIRS), lambda i: (i, 0)),
            pl.BlockSpec(memory_space=pltpu.VMEM),
        ],
        out_specs=pl.BlockSpec((2 * TOK_BLOCK, PEER_PAIRS), lambda i: (i, 0)),
        out_shape=jax.ShapeDtypeStruct((T, PEER_PAIRS), jnp.float32),
        scratch_shapes=_peer_scratch() + [
            pltpu.VMEM((2, PEER_PAIRS, LANES), jnp.float32),
            pltpu.VMEM((PEER_PAIRS, TOK_BLOCK), jnp.float32),
        ],
        compiler_params=pltpu.CompilerParams(
            dimension_semantics=("arbitrary",), vmem_limit_bytes=PEER_VMEM_LIMIT),
        name="peer_dots",
    )(idx.reshape(-1), x.reshape(T, SUBLANES, LANES), gate, tab)


def _peer_mix_kernel(idx_hbm, w_ref, tab_ref, x_ref, g2_ref, gain_ref, bias_ref, out_ref,
                     idx_a, idx_b, sems, m_buf, o_buf, f_buf):
    m4, _, _ = _sublane_masks()
    n_slots = 2 * MIX_UNROLL

    def process(idx_buf, half):
        base = half * TOK_BLOCK

        def spread_weights(tok, slot):
            m = jnp.broadcast_to(w_ref[pl.ds(base + tok, 1), :], (PEER_PAIRS, PEER_PAIRS)).T
            for g in range(PEER_PAIRS // SUBLANES):
                rows = m[g * SUBLANES:(g + 1) * SUBLANES, :]
                for r in range(ROWS_PER_EXPERT):
                    m_buf[slot, pl.ds(g * SUBLANES * ROWS_PER_EXPERT + r, SUBLANES, stride=ROWS_PER_EXPERT), :] = rows

        for tok in range(n_slots):
            spread_weights(tok, tok)

        def token(t, slot, u):
            acc_lo = jnp.zeros((SUBLANES, LANES), jnp.float32)
            acc_hi = jnp.zeros((SUBLANES, LANES), jnp.float32)
            for j in range(PEER_PAIRS // 2):
                wa = tab_ref[pl.ds(_pair_rows(idx_buf, t, 2 * j), SUBLANES), :]
                wb = tab_ref[pl.ds(_pair_rows(idx_buf, t, 2 * j + 1), SUBLANES), :]
                lo, hi = _unpack(jnp.where(m4, wa, wb))
                wt = m_buf[slot, j * SUBLANES:(j + 1) * SUBLANES, :]
                acc_lo = acc_lo + wt * lo
                acc_hi = acc_hi + wt * hi
            lo = acc_lo + pltpu.roll(acc_lo, 4, axis=0)
            hi = acc_hi + pltpu.roll(acc_hi, 4, axis=0)
            o_buf[pl.ds(pl.multiple_of(u * SUBLANES, SUBLANES), SUBLANES), :] = jnp.where(m4, lo, hi)

        def tokens(k, i):
            t0 = k * n_slots
            for grp in range(2):
                slots = range(grp * MIX_UNROLL, (grp + 1) * MIX_UNROLL)
                for s_ in slots:
                    token(t0 + s_, s_, i * n_slots + s_)
                for s_ in slots:
                    spread_weights(jnp.minimum(t0 + n_slots + s_, TOK_BLOCK - 1), s_)

        trips = SUBLANES // n_slots

        def eight_tokens(g8, carry):
            def trip(i, c):
                tokens(g8 * trips + i, i)
                return c

            lax.fori_loop(0, trips, trip, 0)
            row0 = pl.multiple_of(base + g8 * SUBLANES, SUBLANES)
            for r in range(SUBLANES):
                f_buf[pl.ds(row0, SUBLANES), r * LANES:(r + 1) * LANES] = o_buf[pl.ds(r, SUBLANES, stride=SUBLANES), :]
            return carry

        lax.fori_loop(0, TOK_BLOCK // SUBLANES, eight_tokens, 0)

    _two_block_step(idx_hbm, idx_a, idx_b, sems, process)
    y = DEEPNORM_ALPHA * x_ref[...] + g2_ref[0] * f_buf[...]
    out_ref[...] = _norm_rows(y) * gain_ref[...] + bias_ref[...]


def _peer_mix(idx, w, tab, x, g2, gain, bias):
    T, D = x.shape
    blk = 2 * TOK_BLOCK
    blocks_per_seq = T // g2.shape[0] // blk
    return pl.pallas_call(
        _peer_mix_kernel,
        grid=(T // blk,),
        in_specs=[
            pl.BlockSpec(memory_space=pl.ANY),
            pl.BlockSpec((blk, PEER_PAIRS), lambda i: (i, 0)),
            pl.BlockSpec(memory_space=pltpu.VMEM),
            pl.BlockSpec((blk, D), lambda i: (i, 0)),
            pl.BlockSpec((1, 1, D), lambda i: (i // blocks_per_seq, 0, 0)),
            pl.BlockSpec((1, D), lambda i: (0, 0)),
            pl.BlockSpec((1, D), lambda i: (0, 0)),
        ],
        out_specs=pl.BlockSpec((blk, D), lambda i: (i, 0)),
        out_shape=jax.ShapeDtypeStruct((T, D), jnp.float32),
        scratch_shapes=_peer_scratch() + [
            pltpu.VMEM((2 * MIX_UNROLL, ROWS_PER_EXPERT * PEER_PAIRS, LANES), jnp.float32),
            pltpu.VMEM((SUBLANES * SUBLANES, LANES), jnp.float32),
            pltpu.VMEM((blk, D), jnp.float32),
        ],
        compiler_params=pltpu.CompilerParams(
            dimension_semantics=("arbitrary",), vmem_limit_bytes=PEER_VMEM_LIMIT),
        name="peer_mix",
    )(idx.reshape(-1), w, tab, x, g2[:, None, :], gain[None, :], bias[None, :])


def _peer(h, x, g2, gain, bias, wq, subkeys, u_tab, v_tab):
    B, S, D = h.shape
    hf = h.reshape(B * S, D)
    wqt = wq.T.astype(jnp.bfloat16)
    keys = subkeys.reshape(PEER_HEADS * 2, N_KEYS, PEER_HALF).astype(jnp.bfloat16)
    idx_u, idx_v, gate = _peer_router(hf, wqt, keys)
    w = _peer_dots(idx_u, hf, gate, _pack_table(u_tab))
    return _peer_mix(idx_v, w, _pack_table(v_tab), x.reshape(B * S, D), g2, gain, bias).reshape(B, S, D)


def kernel(x, c, positions, w_ada, b_ada, w_in, conv_dw, conv_dw_b, conv_ln_g, conv_ln_b, w_conv_out, b_conv_out, w_out, ln1_g, ln1_b, peer_wq, peer_subkeys, peer_u, peer_v, ln2_g, ln2_b):
    l = 0
    mod = _adaln(c, w_ada[l], b_ada[l])
    sh1, sc1, g1, sh2, sc2, g2 = jnp.split(mod, 6, axis=-1)
    proj = _input_proj(x, sc1, sh1, w_in[l].astype(jnp.bfloat16))
    ret = _retention(proj, positions.astype(jnp.float32)[:, None, :], _retention_consts())
    x1, h2 = _merge(proj, ret, x, conv_dw[l], conv_dw_b[l], conv_ln_g[l], conv_ln_b[l],
                    w_conv_out[l], b_conv_out[l], w_out[l], g1, sc2, sh2, ln1_g[l], ln1_b[l])
    return _peer(h2, x1, g2, ln2_g[l], ln2_b[l], peer_wq[l], peer_subkeys[l], peer_u[l], peer_v[l])
```
